```python
import math
import jax, jax.numpy as jnp
from jax import lax
import numpy as np

D_MODEL = 1024
BATCH = 4
SEQ = 8192
DEPTH = 2

GRID_W = 64
CTX_LEN = 256
A_HEADS = 4
A_HEAD_DIM = 64
A_WIDTH = A_HEADS * 2 * A_HEAD_DIM
B_HEADS = 8
B_HEAD_DIM = 64
B_WIDTH = B_HEADS * B_HEAD_DIM
NA_ROWS = 8
NA_COLS = 16
C_HEADS = 8
C_NOPE = 64
C_ROPE = 32
C_VDIM = 64
C_Q_RANK = 384
C_KV_RANK = 256
C_WIDTH = C_HEADS * C_VDIM
N_BRANCH = 3
IN_SIZES = (A_WIDTH, A_WIDTH, A_WIDTH, B_WIDTH, B_WIDTH, B_WIDTH, C_Q_RANK, C_KV_RANK, C_ROPE, N_BRANCH * D_MODEL)
IN_SPLITS = tuple(sum(IN_SIZES[:i + 1]) for i in range(len(IN_SIZES) - 1))
D_IN = sum(IN_SIZES)
N_EXPERTS = 32
TOP_K = 4
D_FF = D_MODEL
SWIGLU_LIMIT = 7.0
SWIGLU_ALPHA = 1.702
N_MOD = 6
ROPE_BASE = 10000.0
EPS = 1e-6
NEG_INF = -1e30
Q_BLOCK = 128
E_BLOCK = 128

kernel_name = 'hybrid_diffusion_trunk_ctx_prefix'


def rmsnorm(x, g):
    xf = x.astype(jnp.float32)
    y = xf * lax.rsqrt(jnp.mean(xf * xf, axis=-1, keepdims=True) + EPS)
    return (y * g.astype(jnp.float32)).astype(x.dtype)


def modulate(x, g, shift, scale):
    return rmsnorm(x, g) * (1.0 + scale) + shift


def axial_tables(n, rot_dim, dtype):
    pos = jnp.arange(n, dtype=jnp.int32)
    rows = (pos // GRID_W).astype(jnp.float32)
    cols = (pos % GRID_W).astype(jnp.float32)
    n_freq = rot_dim // 4
    inv = ROPE_BASE ** (-jnp.arange(n_freq, dtype=jnp.float32) / n_freq)
    ang = jnp.concatenate([rows[:, None] * inv, cols[:, None] * inv], axis=-1)
    return jnp.cos(ang).astype(dtype), jnp.sin(ang).astype(dtype)


def rope_axial(x, cos, sin):
    d = x.shape[-1]
    qd = d // 4
    shape = (x.shape[1],) + (1,) * (x.ndim - 3) + (2 * qd,)
    cos = cos.reshape(shape)
    sin = sin.reshape(shape)

    def rot(h, cs, sn):
        a, b = h[..., :qd], h[..., qd:]
        return jnp.concatenate([a * cs - b * sn, b * cs + a * sn], axis=-1)

    return jnp.concatenate([rot(x[..., :2 * qd], cos[..., :qd], sin[..., :qd]),
                            rot(x[..., 2 * qd:], cos[..., qd:], sin[..., qd:])], axis=-1)


def blocked_attention(q, k, v, scale):
    bsz, nq, nh, dk = q.shape
    qb = min(Q_BLOCK, nq)
    q_blocks = jnp.moveaxis(q.reshape(bsz, nq // qb, qb, nh, dk), 1, 0)

    def one(qblk):
        s = jnp.einsum('bqhd,bkhd->bhqk', qblk, k).astype(jnp.float32) * scale
        p = jax.nn.softmax(s, axis=-1).astype(v.dtype)
        return jnp.einsum('bhqk,bkhe->bqhe', p, v)

    out = lax.map(one, q_blocks)
    return jnp.moveaxis(out, 0, 1).reshape(bsz, nq, nh, v.shape[-1])


def blocked_diff_attention(q, k, v, lam, scale):
    bsz, nq = q.shape[0], q.shape[1]
    qb = min(Q_BLOCK, nq)
    q_blocks = jnp.moveaxis(q.reshape((bsz, nq // qb, qb) + q.shape[2:]), 1, 0)

    def one(qblk):
        s = jnp.einsum('bqhcd,bkhcd->bhcqk', qblk, k).astype(jnp.float32) * scale
        p = jax.nn.softmax(s, axis=-1)
        a = (p[:, :, 0] - lam * p[:, :, 1]).astype(v.dtype)
        return jnp.einsum('bhqk,bkhe->bqhe', a, v)

    out = lax.map(one, q_blocks)
    return jnp.moveaxis(out, 0, 1).reshape((bsz, nq) + v.shape[2:])


def neighbourhood_attention(q, k, v, k_ctx, v_ctx, rpb):
    bsz, n, nh, d = q.shape
    rows = n // GRID_W
    win_r = min(NA_ROWS, rows)
    scale = d ** -0.5
    qg = q.reshape(bsz, rows, GRID_W, nh, d)
    kg = k.reshape(bsz, rows, GRID_W, nh, d)
    vg = v.reshape(bsz, rows, GRID_W, nh, d)
    r_idx = jnp.arange(rows, dtype=jnp.int32)
    row_start = jnp.clip(r_idx - NA_ROWS // 2, 0, rows - win_r)
    c_idx = jnp.arange(GRID_W, dtype=jnp.int32)
    col_start = jnp.clip(c_idx - NA_COLS // 2, 0, GRID_W - NA_COLS)
    col_mask = (c_idx[None, :] >= col_start[:, None]) & (c_idx[None, :] < col_start[:, None] + NA_COLS)
    col_bias_idx = jnp.clip(c_idx[None, :] - c_idx[:, None] + NA_COLS - 1, 0, 2 * NA_COLS - 2)
    rpb_c = rpb[:, :, col_bias_idx]
    n_lat = win_r * GRID_W

    def one(args):
        q_r, r, rs = args
        k_blk = lax.dynamic_slice_in_dim(kg, rs, win_r, axis=1)
        v_blk = lax.dynamic_slice_in_dim(vg, rs, win_r, axis=1)
        bias = jnp.take(rpb_c, rs + jnp.arange(win_r) - r + NA_ROWS - 1, axis=1)
        s = jnp.einsum('bqhd,bikhd->bhqik', q_r, k_blk).astype(jnp.float32) * scale
        s = s + jnp.transpose(bias, (0, 2, 1, 3))[None].astype(jnp.float32)
        s = jnp.where(col_mask[:, None, :], s, NEG_INF)
        s_ctx = jnp.einsum('bqhd,bkhd->bhqk', q_r, k_ctx).astype(jnp.float32) * scale
        s_all = jnp.concatenate([s.reshape(bsz, nh, GRID_W, n_lat), s_ctx], axis=-1)
        p = jax.nn.softmax(s_all, axis=-1).astype(v.dtype)
        p_lat = p[..., :n_lat].reshape(bsz, nh, GRID_W, win_r, GRID_W)
        return (jnp.einsum('bhqik,bikhe->bqhe', p_lat, v_blk)
                + jnp.einsum('bhqk,bkhe->bqhe', p[..., n_lat:], v_ctx))

    out = lax.map(one, (jnp.moveaxis(qg, 1, 0), r_idx, row_start))
    return jnp.moveaxis(out, 0, 1).reshape(bsz, n, nh, d)


def diff_qkv(qa, ka, va, rope):
    q = qa.reshape(qa.shape[:-1] + (A_HEADS, 2, A_HEAD_DIM))
    k = ka.reshape(ka.shape[:-1] + (A_HEADS, 2, A_HEAD_DIM))
    v = va.reshape(va.shape[:-1] + (A_HEADS, 2 * A_HEAD_DIM))
    if rope is not None:
        q = rope_axial(q, *rope)
        k = rope_axial(k, *rope)
    return q, k, v


def diff_post(o, g_subln, lam_init):
    o = rmsnorm(o, g_subln) * (1.0 - lam_init)
    return o.reshape(o.shape[:-2] + (A_WIDTH,))


def heads(t, nh):
    return t.reshape(t.shape[:-1] + (nh, t.shape[-1] // nh))


def mla_qkv(q_lat, kv_lat, k_rope, g_q_a, w_q_b, g_kv_a, w_kv_b, rope):
    q = heads(rmsnorm(q_lat, g_q_a) @ w_q_b, C_HEADS)
    kv = heads(rmsnorm(kv_lat, g_kv_a) @ w_kv_b, C_HEADS)
    q_nope, q_pe = q[..., :C_NOPE], q[..., C_NOPE:]
    k_nope, v = kv[..., :C_NOPE], kv[..., C_NOPE:]
    k_pe = k_rope[:, :, None, :]
    if rope is not None:
        q_pe = rope_axial(q_pe, *rope)
        k_pe = rope_axial(k_pe, *rope)
    k_pe = jnp.broadcast_to(k_pe, k_nope.shape[:-1] + (C_ROPE,))
    return (jnp.concatenate([q_nope, q_pe], axis=-1), jnp.concatenate([k_nope, k_pe], axis=-1), v)


def merge_branches(ya, yb, yc, gates, w_br_a, w_br_b, w_br_c, w_out):
    g = jax.nn.sigmoid(gates.astype(jnp.float32)).astype(ya.dtype)
    g = g.reshape(g.shape[:-1] + (N_BRANCH, D_MODEL))
    m = g[..., 0, :] * (ya @ w_br_a) + g[..., 1, :] * (yb @ w_br_b) + g[..., 2, :] * (yc @ w_br_c)
    return m @ w_out


def moe_ffn(h, w_router, b_router, w_gate_up, b_gate_up, w_down, b_down):
    shp = h.shape
    ht = h.reshape(-1, shp[-1])
    n_tok = ht.shape[0]
    logits = (ht @ w_router).astype(jnp.float32) + b_router.astype(jnp.float32)
    top_val, top_idx = lax.top_k(logits, TOP_K)
    gate_w = jax.nn.softmax(top_val, axis=-1)
    m = n_tok * TOP_K
    flat_e = top_idx.reshape(m)
    flat_t = jnp.repeat(jnp.arange(n_tok, dtype=jnp.int32), TOP_K)
    order = jnp.argsort(flat_e)
    e_s, t_s, w_s = flat_e[order], flat_t[order], gate_w.reshape(m)[order]
    counts = jnp.zeros((N_EXPERTS,), jnp.int32).at[flat_e].add(1)
    start = jnp.cumsum(counts) - counts
    padded = ((counts + E_BLOCK - 1) // E_BLOCK) * E_BLOCK
    pend = jnp.cumsum(padded)
    pstart = pend - padded
    dest = pstart[e_s] + jnp.arange(m, dtype=jnp.int32) - start[e_s]
    n_blk = -(-m // E_BLOCK) + N_EXPERTS
    n_rows = n_blk * E_BLOCK
    buf_t = jnp.full((n_rows,), n_tok, jnp.int32).at[dest].set(t_s)
    buf_w = jnp.zeros((n_rows,), jnp.float32).at[dest].set(w_s)
    blk_e = jnp.clip(jnp.searchsorted(pend, jnp.arange(n_blk, dtype=jnp.int32) * E_BLOCK, side='right'), 0, N_EXPERTS - 1)
    x_pad = jnp.concatenate([ht, jnp.zeros((1, ht.shape[-1]), ht.dtype)], axis=0)

    def expert_block(args):
        tok, e = args
        gu = x_pad[tok] @ w_gate_up[e] + b_gate_up[e]
        gate = jnp.minimum(gu[:, 0::2], SWIGLU_LIMIT)
        up = jnp.clip(gu[:, 1::2], -SWIGLU_LIMIT, SWIGLU_LIMIT)
        act = (up + 1.0) * (gate * jax.nn.sigmoid(SWIGLU_ALPHA * gate))
        return act @ w_down[e] + b_down[e]

    y = lax.map(expert_block, (buf_t.reshape(n_blk, E_BLOCK), blk_e)).reshape(n_rows, -1)
    out = jax.ops.segment_sum(y * buf_w[:, None].astype(y.dtype), buf_t, num_segments=n_tok + 1)[:n_tok]
    return out.reshape(shp)


def setup_inputs(seed: int = 0) -> dict:
    key = jax.random.key(seed)
    ks = jax.random.split(key, 30)
    f32 = jnp.float32

    def nrm(k, shape, scale):
        return jax.random.normal(k, shape, f32) * scale

    def gain(k, shape):
        return 1.0 + 0.05 * jax.random.normal(k, shape, f32)

    L = DEPTH
    return {
        'x': nrm(ks[0], (BATCH, SEQ, D_MODEL), 1.0),
        'c': nrm(ks[1], (BATCH, D_MODEL), 1.0),
        'ctx': nrm(ks[2], (BATCH, CTX_LEN, D_MODEL), 1.0),
        'c_ctx': nrm(ks[3], (D_MODEL,), 1.0),
        'w_mod': nrm(ks[4], (L, D_MODEL, N_MOD * D_MODEL), 0.5 * D_MODEL ** -0.5),
        'b_mod': nrm(ks[5], (L, N_MOD * D_MODEL), 0.01),
        'g_mix': gain(ks[6], (L, D_MODEL)),
        'w_in': nrm(ks[7], (L, D_MODEL, D_IN), D_MODEL ** -0.5),
        'lam_q1': nrm(ks[8], (L, A_HEAD_DIM), 0.1),
        'lam_k1': nrm(ks[9], (L, A_HEAD_DIM), 0.1),
        'lam_q2': nrm(ks[10], (L, A_HEAD_DIM), 0.1),
        'lam_k2': nrm(ks[11], (L, A_HEAD_DIM), 0.1),
        'g_subln': gain(ks[12], (L, 2 * A_HEAD_DIM)),
        'rpb': nrm(ks[13], (L, B_HEADS, 2 * NA_ROWS - 1, 2 * NA_COLS - 1), 0.1),
        'g_q_a': gain(ks[14], (L, C_Q_RANK)),
        'w_q_b': nrm(ks[15], (L, C_Q_RANK, C_HEADS * (C_NOPE + C_ROPE)), C_Q_RANK ** -0.5),
        'g_kv_a': gain(ks[16], (L, C_KV_RANK)),
        'w_kv_b': nrm(ks[17], (L, C_KV_RANK, C_HEADS * (C_NOPE + C_VDIM)), C_KV_RANK ** -0.5),
        'w_br_a': nrm(ks[18], (L, A_WIDTH, D_MODEL), A_WIDTH ** -0.5),
        'w_br_b': nrm(ks[19], (L, B_WIDTH, D_MODEL), B_WIDTH ** -0.5),
        'w_br_c': nrm(ks[20], (L, C_WIDTH, D_MODEL), C_WIDTH ** -0.5),
        'w_out': nrm(ks[21], (L, D_MODEL, D_MODEL), D_MODEL ** -0.5),
        'g_ffn': gain(ks[22], (L, D_MODEL)),
        'w_router': nrm(ks[23], (L, D_MODEL, N_EXPERTS), D_MODEL ** -0.5),
        'b_router': nrm(ks[24], (L, N_EXPERTS), 0.01),
        'w_gate_up': nrm(ks[25], (L, N_EXPERTS, D_MODEL, 2 * D_FF), D_MODEL ** -0.5),
        'b_gate_up': nrm(ks[26], (L, N_EXPERTS, 2 * D_FF), 0.01),
        'w_down': nrm(ks[27], (L, N_EXPERTS, D_FF, D_MODEL), D_FF ** -0.5),
        'b_down': nrm(ks[28], (L, N_EXPERTS, D_MODEL), 0.01),
        'g_final': gain(ks[29], (D_MODEL,)),
    }


def reference(x, c, ctx, c_ctx, w_mod, b_mod, g_mix, w_in, lam_q1, lam_k1, lam_q2, lam_k2, g_subln, rpb,
              g_q_a, w_q_b, g_kv_a, w_kv_b, w_br_a, w_br_b, w_br_c, w_out, g_ffn, w_router, b_router,
              w_gate_up, b_gate_up, w_down, b_down, g_final):
    n = x.shape[1]
    rope_a = axial_tables(n, A_HEAD_DIM, x.dtype)
    rope_c = axial_tables(n, C_ROPE, x.dtype)
    scale_a = A_HEAD_DIM ** -0.5
    scale_b = B_HEAD_DIM ** -0.5
    scale_c = (C_NOPE + C_ROPE) ** -0.5
    xc = ctx
    for l in range(DEPTH):
        last = l == DEPTH - 1
        lam_init = 0.8 - 0.6 * math.exp(-0.3 * l)
        lam = (jnp.exp(jnp.sum(lam_q1[l].astype(jnp.float32) * lam_k1[l].astype(jnp.float32)))
               - jnp.exp(jnp.sum(lam_q2[l].astype(jnp.float32) * lam_k2[l].astype(jnp.float32))) + lam_init)
        mod = (jax.nn.silu(c) @ w_mod[l] + b_mod[l])[:, None, :]
        mod_c = jax.nn.silu(c_ctx) @ w_mod[l] + b_mod[l]
        sh1, sc1, gt1, sh2, sc2, gt2 = jnp.split(mod, N_MOD, axis=-1)
        csh1, csc1, cgt1, csh2, csc2, cgt2 = jnp.split(mod_c, N_MOD, axis=-1)
        h = modulate(x, g_mix[l], sh1, sc1)
        hc = modulate(xc, g_mix[l], csh1, csc1)
        qa, ka, va, qb, kb, vb, cq, ckv, ckr, gates = jnp.split(h @ w_in[l], IN_SPLITS, axis=-1)
        qa_c, ka_c, va_c, qb_c, kb_c, vb_c, cq_c, ckv_c, ckr_c, gates_c = jnp.split(hc @ w_in[l], IN_SPLITS, axis=-1)
        qA, kA, vA = diff_qkv(qa, ka, va, rope_a)
        qAc, kAc, vAc = diff_qkv(qa_c, ka_c, va_c, None)
        oA = blocked_diff_attention(qA, jnp.concatenate([kA, kAc], axis=1), jnp.concatenate([vA, vAc], axis=1), lam, scale_a)
        qB, kB, vB = heads(qb, B_HEADS), heads(kb, B_HEADS), heads(vb, B_HEADS)
        qBc, kBc, vBc = heads(qb_c, B_HEADS), heads(kb_c, B_HEADS), heads(vb_c, B_HEADS)
        oB = neighbourhood_attention(qB, kB, vB, kBc, vBc, rpb[l])
        qC, kC, vC = mla_qkv(cq, ckv, ckr, g_q_a[l], w_q_b[l], g_kv_a[l], w_kv_b[l], rope_c)
        qCc, kCc, vCc = mla_qkv(cq_c, ckv_c, ckr_c, g_q_a[l], w_q_b[l], g_kv_a[l], w_kv_b[l], None)
        oC = blocked_attention(qC, jnp.concatenate([kC, kCc], axis=1), jnp.concatenate([vC, vCc], axis=1), scale_c)
        y = merge_branches(diff_post(oA, g_subln[l], lam_init), oB.reshape(oB.shape[:2] + (B_WIDTH,)),
                           oC.reshape(oC.shape[:2] + (C_WIDTH,)), gates, w_br_a[l], w_br_b[l], w_br_c[l], w_out[l])
        x = x + gt1 * y
        x = x + gt2 * moe_ffn(modulate(x, g_ffn[l], sh2, sc2), w_router[l], b_router[l],
                              w_gate_up[l], b_gate_up[l], w_down[l], b_down[l])
        if not last:
            oAc = blocked_diff_attention(qAc, kAc, vAc, lam, scale_a)
            oBc = blocked_attention(qBc, kBc, vBc, scale_b)
            oCc = blocked_attention(qCc, kCc, vCc, scale_c)
            yc = merge_branches(diff_post(oAc, g_subln[l], lam_init), oBc.reshape(oBc.shape[:2] + (B_WIDTH,)),
                                oCc.reshape(oCc.shape[:2] + (C_WIDTH,)), gates_c, w_br_a[l], w_br_b[l], w_br_c[l], w_out[l])
            xc = xc + cgt1 * yc
            xc = xc + cgt2 * moe_ffn(modulate(xc, g_ffn[l], csh2, csc2), w_router[l], b_router[l],
                                     w_gate_up[l], b_gate_up[l], w_down[l], b_down[l])
    return rmsnorm(x, g_final)
```

```python
import functools
import math

import jax
import jax.numpy as jnp
from jax import lax
from jax.experimental import pallas as pl
from jax.experimental.pallas import tpu as pltpu

GRID_W = 64
A_HEADS = 4
A_HEAD_DIM = 64
A_WIDTH = A_HEADS * 2 * A_HEAD_DIM
B_HEADS = 8
B_HEAD_DIM = 64
B_WIDTH = B_HEADS * B_HEAD_DIM
NA_ROWS = 8
NA_COLS = 16
C_HEADS = 8
C_NOPE = 64
C_ROPE = 32
C_VDIM = 64
C_Q_RANK = 384
C_KV_RANK = 256
C_WIDTH = C_HEADS * C_VDIM
N_BRANCH = 3
N_EXPERTS = 32
TOP_K = 4
SWIGLU_LIMIT = 7.0
SWIGLU_ALPHA = 1.702
N_MOD = 6
ROPE_BASE = 10000.0
EPS = 1e-6
NEG_INF = -1e30

LANES = 128
VMEM_LIMIT = 56 * 1024 * 1024
TM = 256
TQ = 256
TK = 512
MOE_TB = 256
NB_ROWS_PER_STEP = 8

BF16 = jnp.bfloat16
F32 = jnp.float32


def _cparams(sem):
    return pltpu.CompilerParams(dimension_semantics=sem, vmem_limit_bytes=VMEM_LIMIT)


def _dot(a, b):
    return jnp.dot(a, b, preferred_element_type=F32)


def _dot_nt(a, b):
    return lax.dot_general(a, b, (((1,), (1,)), ((), ())), preferred_element_type=F32)


def _split_bf16(a):
    hi = a.astype(BF16)
    lo = (a - hi.astype(F32)).astype(BF16)
    return hi, lo


def _dot_split(a, b):
    a_hi, a_lo = _split_bf16(a)
    b_hi, b_lo = _split_bf16(b)
    return _dot(a_hi, b_hi) + _dot(a_hi, b_lo) + _dot(a_lo, b_hi)


def _rms(x):
    return x * lax.rsqrt(jnp.mean(x * x, axis=-1, keepdims=True) + EPS)


def _modulated(x_ref, g_ref, sh_ref, sc_ref):
    return (_rms(x_ref[...]) * g_ref[...]) * (1.0 + sc_ref[0]) + sh_ref[0]


def _rope(blk, cos, sin_signed, shift):
    lane = lax.broadcasted_iota(jnp.int32, blk.shape, 1)
    first = ((lane // shift) % 2) == 0
    partner = jnp.where(first, pltpu.roll(blk, LANES - shift, 1), pltpu.roll(blk, shift, 1))
    return blk * cos + partner * sin_signed


def _mod_kernel(c_ref, w_ref, b_ref, o_ref):
    cc = c_ref[...]
    a = cc * jax.nn.sigmoid(cc)
    o_ref[...] = _dot_split(a, w_ref[...]) + b_ref[...]


def _mod_call(c8, w_mod, b_mod):
    d = c8.shape[1]
    n_out = w_mod.shape[1]
    return pl.pallas_call(
        _mod_kernel,
        grid=(n_out // d,),
        in_specs=[pl.BlockSpec((8, d), lambda j: (0, 0)),
                  pl.BlockSpec((d, d), lambda j: (0, j)),
                  pl.BlockSpec((1, d), lambda j: (0, j))],
        out_specs=pl.BlockSpec((8, d), lambda j: (0, j)),
        out_shape=jax.ShapeDtypeStruct((8, n_out), F32),
        compiler_params=_cparams(("arbitrary",)),
        name="mod",
    )(c8, w_mod, b_mod.reshape(1, n_out))


class _Rows:
    def __init__(self, bsz, n, n_ctx):
        assert n % TM == 0 and (bsz * n_ctx) % TM == 0
        self.bsz, self.n, self.n_ctx = bsz, n, n_ctx
        self.t_lat = bsz * n
        self.t_all = bsz * n + bsz * n_ctx
        self.tiles_per_batch = n // TM
        self.lat_tiles = self.t_lat // TM
        self.all_tiles = self.t_all // TM

    def mod_spec(self, which, d):
        tpb, bsz = self.tiles_per_batch, self.bsz
        return pl.BlockSpec((1, 1, d), lambda i, *_: (jnp.minimum(i // tpb, bsz) * N_MOD + which, 0, 0))

    def rope_spec(self):
        tpb, lat = self.tiles_per_batch, self.lat_tiles
        return pl.BlockSpec((TM, LANES), lambda i, *_: (jnp.where(i < lat, i % tpb, tpb), 0))


def _inproj_qkv_kernel(*refs, rope, qscale):
    if rope:
        x_ref, g_ref, sh_ref, sc_ref, w_ref, cos_ref, sin_ref, q_ref, k_ref, v_ref = refs
    else:
        x_ref, g_ref, sh_ref, sc_ref, w_ref, q_ref, k_ref, v_ref = refs
    h = _modulated(x_ref, g_ref, sh_ref, sc_ref).astype(BF16)
    t = _dot(h, w_ref[...])
    outs = (q_ref, k_ref, v_ref)
    per = q_ref.shape[1] // LANES
    for j in range(3 * per):
        blk = t[:, j * LANES:(j + 1) * LANES]
        if rope and j < 2 * per:
            blk = _rope(blk, cos_ref[...], sin_ref[...], A_HEAD_DIM // 4)
        if j < per:
            blk = blk * qscale
        outs[j // per][:, (j % per) * LANES:(j % per + 1) * LANES] = blk.astype(BF16)


def _inproj_qkv_call(rows, x_all, g, mod3, w, tables, qscale):
    d = x_all.shape[1]
    width = w.shape[1] // 3
    rope = tables is not None
    in_specs = [pl.BlockSpec((TM, d), lambda i: (i, 0)),
                pl.BlockSpec((1, d), lambda i: (0, 0)),
                rows.mod_spec(0, d), rows.mod_spec(1, d),
                pl.BlockSpec((d, 3 * width), lambda i: (0, 0))]
    args = [x_all, g, mod3, mod3, w]
    if rope:
        in_specs += [rows.rope_spec(), rows.rope_spec()]
        args += list(tables)
    out = jax.ShapeDtypeStruct((rows.t_all, width), BF16)
    return pl.pallas_call(
        functools.partial(_inproj_qkv_kernel, rope=rope, qscale=qscale),
        grid=(rows.all_tiles,),
        in_specs=in_specs,
        out_specs=[pl.BlockSpec((TM, width), lambda i: (i, 0))] * 3,
        out_shape=[out] * 3,
        compiler_params=_cparams(("parallel",)),
        name="inproj_qkv_rope" if rope else "inproj_qkv",
    )(*args)


def _inproj_gates_kernel(x_ref, g_ref, sh_ref, sc_ref, w_ref, o_ref, h_scr):
    @pl.when(pl.program_id(1) == 0)
    def _():
        h_scr[...] = _modulated(x_ref, g_ref, sh_ref, sc_ref).astype(BF16)

    o_ref[...] = jax.nn.sigmoid(_dot(h_scr[...], w_ref[...])).astype(BF16)


def _inproj_gates_call(rows, x_all, g, mod3, w):
    d = x_all.shape[1]
    n_out = w.shape[1]
    return pl.pallas_call(
        _inproj_gates_kernel,
        grid=(rows.all_tiles, n_out // d),
        in_specs=[pl.BlockSpec((TM, d), lambda i, j: (i, 0)),
                  pl.BlockSpec((1, d), lambda i, j: (0, 0)),
                  rows.mod_spec(0, d), rows.mod_spec(1, d),
                  pl.BlockSpec((d, d), lambda i, j: (0, j))],
        out_specs=pl.BlockSpec((TM, d), lambda i, j: (i, j)),
        out_shape=jax.ShapeDtypeStruct((rows.t_all, n_out), BF16),
        scratch_shapes=[pltpu.VMEM((TM, d), BF16)],
        compiler_params=_cparams(("parallel", "arbitrary")),
        name="inproj_gates",
    )(x_all, g, mod3, mod3, w)


def _inproj_mla_kernel(x_ref, g_ref, sh_ref, sc_ref, w_ref, gq_ref, wq_ref, gkv_ref, wkv_ref,
                       cos_ref, sin_ref, q_ref, k_ref, v_ref, *, qscale):
    h = _modulated(x_ref, g_ref, sh_ref, sc_ref).astype(BF16)
    t = _dot(h, w_ref[...])
    cos, sin = cos_ref[...], sin_ref[...]
    shift = C_ROPE // 4
    cq = (_rms(t[:, :C_Q_RANK]) * gq_ref[...]).astype(BF16)
    q = _dot(cq, wq_ref[...])
    for j in range(C_HEADS):
        blk = _rope(q[:, j * LANES:(j + 1) * LANES], cos, sin, shift) * qscale
        q_ref[:, j * LANES:(j + 1) * LANES] = blk.astype(BF16)
    ckv = (_rms(t[:, C_Q_RANK:C_Q_RANK + C_KV_RANK]) * gkv_ref[...]).astype(BF16)
    kv = _dot(ckv, wkv_ref[...])
    pe = _rope(t[:, C_Q_RANK + C_KV_RANK:], cos, sin, shift)
    for j in range(C_HEADS):
        k_ref[:, j * LANES:(j + 1) * LANES] = (kv[:, j * LANES:(j + 1) * LANES] + pe).astype(BF16)
    v_ref[...] = kv[:, C_HEADS * LANES:].astype(BF16)


def _inproj_mla_call(rows, x_all, g, mod3, w, gq, wq, gkv, wkv, tables, qscale):
    d = x_all.shape[1]
    full = lambda a: pl.BlockSpec(a.shape, lambda i: (0,) * a.ndim)
    qk = jax.ShapeDtypeStruct((rows.t_all, C_HEADS * LANES), BF16)
    return pl.pallas_call(
        functools.partial(_inproj_mla_kernel, qscale=qscale),
        grid=(rows.all_tiles,),
        in_specs=[pl.BlockSpec((TM, d), lambda i: (i, 0)), full(g),
                  rows.mod_spec(0, d), rows.mod_spec(1, d),
                  full(w), full(gq), full(wq), full(gkv), full(wkv),
                  rows.rope_spec(), rows.rope_spec()],
        out_specs=[pl.BlockSpec((TM, C_HEADS * LANES), lambda i: (i, 0)),
                   pl.BlockSpec((TM, C_HEADS * LANES), lambda i: (i, 0)),
                   pl.BlockSpec((TM, C_WIDTH), lambda i: (i, 0))],
        out_shape=[qk, qk, jax.ShapeDtypeStruct((rows.t_all, C_WIDTH), BF16)],
        compiler_params=_cparams(("parallel",)),
        name="inproj_mla",
    )(x_all, g, mod3, mod3, w, gq, wq, gkv, wkv, *tables)


def _pair_attention_kernel(*refs, n_lat_chunks, mode, lam_init):
    if n_lat_chunks:
        q_ref, kl_ref, vl_ref, kc_ref, vc_ref, lam_ref, gs_ref, o_ref = refs
    else:
        q_ref, kc_ref, vc_ref, lam_ref, gs_ref, o_ref = refs
    q = q_ref[...]
    tq, wq = q.shape
    lane = lax.broadcasted_iota(jnp.int32, q.shape, 1)
    lower = lane < wq // 2
    zero = jnp.zeros_like(q)
    qs = jnp.concatenate([jnp.where(lower, q, zero), jnp.where(lower, zero, q)], axis=0)

    def step(kc, vc, carry):
        m, l, acc = carry
        s = _dot_nt(qs, kc)
        m_new = jnp.maximum(m, jnp.max(s, axis=-1, keepdims=True))
        p = jnp.exp(s - m_new)
        alpha = jnp.exp(m - m_new)
        l = alpha * l + jnp.sum(p, axis=-1, keepdims=True)
        acc = alpha * acc + _dot(p.astype(BF16), vc)
        return m_new, l, acc

    carry = (jnp.full((2 * tq, 1), NEG_INF, F32), jnp.zeros((2 * tq, 1), F32),
             jnp.zeros((2 * tq, LANES), F32))
    if n_lat_chunks:
        def body(i, c):
            off = pl.multiple_of(i * TK, TK)
            return step(kl_ref[pl.ds(off, TK), :], vl_ref[pl.ds(off, TK), :], c)
        carry = lax.fori_loop(0, n_lat_chunks, body, carry)
    m, l, acc = step(kc_ref[...], vc_ref[...], carry)
    o = acc / l
    o0, o1 = o[:tq], o[tq:]
    if mode == "diff":
        lv = lam_ref[...]
        lam = (jnp.exp(jnp.sum(lv[0:1] * lv[1:2], axis=-1, keepdims=True))
               - jnp.exp(jnp.sum(lv[2:3] * lv[3:4], axis=-1, keepdims=True)) + lam_init)
        out = (_rms(o0 - lam * o1) * gs_ref[...]) * (1.0 - lam_init)
    else:
        out = jnp.where(lax.broadcasted_iota(jnp.int32, o0.shape, 1) < LANES // 2, o0, o1)
    o_ref[...] = out.astype(BF16)


def _pair_attention_call(rows, q, k, v, lamv, gs, *, wq, mode, lam_init, ctx_queries):
    bsz, n, n_ctx = rows.bsz, rows.n, rows.n_ctx
    n_pairs = v.shape[1] // LANES
    ctx_blk0 = rows.t_lat // n_ctx
    kc_spec = pl.BlockSpec((n_ctx, wq), lambda b, h, i: (ctx_blk0 + b, h))
    vc_spec = pl.BlockSpec((n_ctx, LANES), lambda b, h, i: (ctx_blk0 + b, h))
    small = [pl.BlockSpec(lamv.shape, lambda b, h, i: (0, 0)), pl.BlockSpec(gs.shape, lambda b, h, i: (0, 0))]
    if ctx_queries:
        tq, n_q, n_chunks = n_ctx, 1, 0
        q_spec = pl.BlockSpec((tq, wq), lambda b, h, i: (ctx_blk0 + b, h))
        o_spec = pl.BlockSpec((tq, LANES), lambda b, h, i: (b, h))
        in_specs = [q_spec, kc_spec, vc_spec] + small
        args = (q, k, v, lamv, gs)
        out_rows = bsz * n_ctx
    else:
        assert n % TQ == 0 and n % TK == 0
        tq, n_q, n_chunks = TQ, n // TQ, n // TK
        q_spec = pl.BlockSpec((tq, wq), lambda b, h, i: (b * n_q + i, h))
        o_spec = pl.BlockSpec((tq, LANES), lambda b, h, i: (b * n_q + i, h))
        in_specs = [q_spec,
                    pl.BlockSpec((n, wq), lambda b, h, i: (b, h)),
                    pl.BlockSpec((n, LANES), lambda b, h, i: (b, h)),
                    kc_spec, vc_spec] + small
        args = (q, k, v, k, v, lamv, gs)
        out_rows = rows.t_lat
    return pl.pallas_call(
        functools.partial(_pair_attention_kernel, n_lat_chunks=n_chunks, mode=mode, lam_init=lam_init),
        grid=(bsz, n_pairs, n_q),
        in_specs=in_specs,
        out_specs=o_spec,
        out_shape=jax.ShapeDtypeStruct((out_rows, n_pairs * LANES), BF16),
        compiler_params=_cparams(("parallel", "parallel", "arbitrary")),
        name=f"attn_{mode}_{'ctx' if ctx_queries else 'lat'}_{wq}",
    )(*args)


def _nbr_classes(n_rows):
    half = NA_ROWS // 2
    reps = list(range(half)) + [half] + list(range(n_rows - half + 1, n_rows))
    return reps


def _nbr_bias_table(rpb, n_rows):
    c_idx = jnp.arange(GRID_W, dtype=jnp.int32)
    col_start = jnp.clip(c_idx - NA_COLS // 2, 0, GRID_W - NA_COLS)
    col_mask = (c_idx[None, :] >= col_start[:, None]) & (c_idx[None, :] < col_start[:, None] + NA_COLS)
    col_bias_idx = jnp.clip(c_idx[None, :] - c_idx[:, None] + NA_COLS - 1, 0, 2 * NA_COLS - 2)
    rpb_c = rpb[:, :, col_bias_idx]
    tabs = []
    for r in _nbr_classes(n_rows):
        rs = min(max(r - NA_ROWS // 2, 0), n_rows - NA_ROWS)
        di = rs + jnp.arange(NA_ROWS) - r + NA_ROWS - 1
        bias = jnp.transpose(rpb_c[:, di], (0, 2, 1, 3))
        bias = jnp.where(col_mask[None, :, None, :], bias, NEG_INF)
        tabs.append(bias.reshape(rpb.shape[0], GRID_W, NA_ROWS * GRID_W))
    return jnp.stack(tabs).astype(F32)


def _nbr_attention_kernel(q_ref, kl_ref, vl_ref, kc_ref, vc_ref, bias_ref, o_ref, *, n_rows):
    half = NA_ROWS // 2
    kc, vc = kc_ref[...], vc_ref[...]
    win = NA_ROWS * GRID_W

    def one_row(j, _):
        r = pl.program_id(2) * NB_ROWS_PER_STEP + j
        rs = jnp.clip(r - half, 0, n_rows - NA_ROWS)
        cls = jnp.minimum(r, half) + jnp.maximum(r - (n_rows - half), 0)
        q = q_ref[pl.ds(pl.multiple_of(j * GRID_W, GRID_W), GRID_W), :]
        lane = lax.broadcasted_iota(jnp.int32, q.shape, 1)
        lower = lane < LANES // 2
        zero = jnp.zeros_like(q)
        qs = jnp.concatenate([jnp.where(lower, q, zero), jnp.where(lower, zero, q)], axis=0)
        off = pl.multiple_of(rs * GRID_W, GRID_W)
        kw = kl_ref[pl.ds(off, win), :]
        vw = vl_ref[pl.ds(off, win), :]
        bias = bias_ref[cls].reshape(2 * GRID_W, win)
        s = jnp.where(bias > 0.5 * NEG_INF, _dot_nt(qs, kw) + bias, NEG_INF)
        s_ctx = _dot_nt(qs, kc)
        m = jnp.maximum(jnp.max(s, axis=-1, keepdims=True), jnp.max(s_ctx, axis=-1, keepdims=True))
        p = jnp.exp(s - m)
        p_ctx = jnp.exp(s_ctx - m)
        l = jnp.sum(p, axis=-1, keepdims=True) + jnp.sum(p_ctx, axis=-1, keepdims=True)
        o = (_dot(p.astype(BF16), vw) + _dot(p_ctx.astype(BF16), vc)) / l
        out = jnp.where(lax.broadcasted_iota(jnp.int32, (GRID_W, LANES), 1) < LANES // 2,
                        o[:GRID_W], o[GRID_W:])
        o_ref[pl.ds(pl.multiple_of(j * GRID_W, GRID_W), GRID_W), :] = out.astype(BF16)
        return 0

    lax.fori_loop(0, NB_ROWS_PER_STEP, one_row, 0)


def _nbr_attention_call(rows, q, k, v, bias):
    bsz, n, n_ctx = rows.bsz, rows.n, rows.n_ctx
    n_rows = n // GRID_W
    assert n_rows % NB_ROWS_PER_STEP == 0 and n_rows >= 2 * NA_ROWS
    steps = n_rows // NB_ROWS_PER_STEP
    n_pairs = B_HEADS // 2
    ctx_blk0 = rows.t_lat // n_ctx
    tq = NB_ROWS_PER_STEP * GRID_W
    n_cls = bias.shape[0]
    return pl.pallas_call(
        functools.partial(_nbr_attention_kernel, n_rows=n_rows),
        grid=(bsz, n_pairs, steps),
        in_specs=[pl.BlockSpec((tq, LANES), lambda b, h, i: (b * steps + i, h)),
                  pl.BlockSpec((n, LANES), lambda b, h, i: (b, h)),
                  pl.BlockSpec((n, LANES), lambda b, h, i: (b, h)),
                  pl.BlockSpec((n_ctx, LANES), lambda b, h, i: (ctx_blk0 + b, h)),
                  pl.BlockSpec((n_ctx, LANES), lambda b, h, i: (ctx_blk0 + b, h)),
                  pl.BlockSpec((n_cls, 2, GRID_W, NA_ROWS * GRID_W), lambda b, h, i: (0, h, 0, 0))],
        out_specs=pl.BlockSpec((tq, LANES), lambda b, h, i: (b * steps + i, h)),
        out_shape=jax.ShapeDtypeStruct((rows.t_lat, B_WIDTH), BF16),
        compiler_params=_cparams(("parallel", "parallel", "arbitrary")),
        name="attn_nbr",
    )(q, k, v, k, v, bias)


def _merge_kernel(ya_ref, yb_ref, yc_ref, g_ref, x_ref, wa_ref, wb_ref, wc_ref, wo_ref,
                  gt1_ref, gffn_ref, sh2_ref, sc2_ref, wr_ref, br_ref,
                  x1_ref, h2_ref, ri_ref, rw_ref):
    d = x_ref.shape[1]
    g = g_ref[...].astype(F32)
    m = (g[:, :d] * _dot(ya_ref[...], wa_ref[...])
         + g[:, d:2 * d] * _dot(yb_ref[...], wb_ref[...])
         + g[:, 2 * d:] * _dot(yc_ref[...], wc_ref[...]))
    y = _dot(m.astype(BF16), wo_ref[...])
    x1 = x_ref[...] + gt1_ref[0] * y
    x1_ref[...] = x1
    h2 = (_rms(x1) * gffn_ref[...]) * (1.0 + sc2_ref[0]) + sh2_ref[0]
    h2_ref[...] = h2.astype(BF16)
    logits = _dot_split(h2, wr_ref[...]) + br_ref[...]
    lane = lax.broadcasted_iota(jnp.int32, logits.shape, 1)
    lane_f = lane.astype(F32)
    work = jnp.where(lane < N_EXPERTS, logits, NEG_INF)
    idx_out = jnp.zeros(logits.shape, F32)
    val_out = jnp.zeros(logits.shape, F32)
    top = None
    den = jnp.zeros((logits.shape[0], 1), F32)
    for j in range(TOP_K):
        mx = jnp.max(work, axis=-1, keepdims=True)
        idx = jnp.min(jnp.where(work == mx, lane_f, float(LANES)), axis=-1, keepdims=True)
        if top is None:
            top = mx
        e = jnp.exp(mx - top)
        den = den + e
        idx_out = jnp.where(lane == j, idx, idx_out)
        val_out = jnp.where(lane == j, e, val_out)
        work = jnp.where(lane_f == idx, NEG_INF, work)
    ri_ref[...] = idx_out.astype(jnp.int32)
    rw_ref[...] = val_out / den


def _merge_call(rows, n_tiles, ya, yb, yc, gates, x_all, wa, wb, wc, wo, mod3, gffn, wr, br):
    d = x_all.shape[1]
    t = n_tiles * TM
    full = lambda a: pl.BlockSpec(a.shape, lambda i: (0,) * a.ndim)
    row = lambda w: pl.BlockSpec((TM, w), lambda i: (i, 0))
    return pl.pallas_call(
        _merge_kernel,
        grid=(n_tiles,),
        in_specs=[row(ya.shape[1]), row(yb.shape[1]), row(yc.shape[1]), row(gates.shape[1]), row(d),
                  full(wa), full(wb), full(wc), full(wo),
                  rows.mod_spec(2, d), full(gffn), rows.mod_spec(3, d), rows.mod_spec(4, d),
                  full(wr), full(br)],
        out_specs=[row(d), row(d), row(LANES), row(LANES)],
        out_shape=[jax.ShapeDtypeStruct((t, d), F32), jax.ShapeDtypeStruct((t, d), BF16),
                   jax.ShapeDtypeStruct((t, LANES), jnp.int32), jax.ShapeDtypeStruct((t, LANES), F32)],
        compiler_params=_cparams(("parallel",)),
        name="merge_router",
    )(ya, yb, yc, gates, x_all, wa, wb, wc, wo, mod3, gffn, mod3, mod3, wr, br)


def _moe_kernel(blk_e_ref, n_used_ref, x_ref, wg_ref, bg_ref, wu_ref, bu_ref, wd_ref, bd_ref, rw_ref, o_ref):
    i = pl.program_id(0)

    @pl.when(i < n_used_ref[0])
    def _():
        x = x_ref[...]
        gate = jnp.minimum(_dot(x, wg_ref[0]) + bg_ref[0], SWIGLU_LIMIT)
        up = jnp.clip(_dot(x, wu_ref[0]) + bu_ref[0], -SWIGLU_LIMIT, SWIGLU_LIMIT)
        act = (up + 1.0) * (gate * jax.nn.sigmoid(SWIGLU_ALPHA * gate))
        y = _dot(act.astype(BF16), wd_ref[0]) + bd_ref[0]
        o_ref[...] = y * rw_ref[...]

    @pl.when(i >= n_used_ref[0])
    def _():
        o_ref[...] = jnp.zeros_like(o_ref)


def _moe_call(xg, row_w, blk_e, n_used, wg, bg, wu, bu, wd, bd):
    n_rows, d = xg.shape
    f = wg.shape[2]
    n_blk = n_rows // MOE_TB
    last = lambda i, be, nu: jnp.minimum(i, nu[0] - 1)
    grid_spec = pltpu.PrefetchScalarGridSpec(
        num_scalar_prefetch=2,
        grid=(n_blk,),
        in_specs=[pl.BlockSpec((MOE_TB, d), lambda i, be, nu: (last(i, be, nu), 0)),
                  pl.BlockSpec((1, d, f), lambda i, be, nu: (be[i], 0, 0)),
                  pl.BlockSpec((1, 1, f), lambda i, be, nu: (be[i], 0, 0)),
                  pl.BlockSpec((1, d, f), lambda i, be, nu: (be[i], 0, 0)),
                  pl.BlockSpec((1, 1, f), lambda i, be, nu: (be[i], 0, 0)),
                  pl.BlockSpec((1, f, d), lambda i, be, nu: (be[i], 0, 0)),
                  pl.BlockSpec((1, 1, d), lambda i, be, nu: (be[i], 0, 0)),
                  pl.BlockSpec((MOE_TB, 1), lambda i, be, nu: (last(i, be, nu), 0))],
        out_specs=pl.BlockSpec((MOE_TB, d), lambda i, be, nu: (i, 0)),
    )
    return pl.pallas_call(
        _moe_kernel,
        grid_spec=grid_spec,
        out_shape=jax.ShapeDtypeStruct((n_rows, d), F32),
        compiler_params=_cparams(("arbitrary",)),
        name="moe_experts",
    )(blk_e, n_used, xg, wg, bg, wu, bu, wd, bd, row_w)


def _combine_kernel(x_ref, y_ref, gt2_ref, gfin_ref, o_ref, *, final):
    d = x_ref.shape[1]
    y = y_ref[...]
    s = y[:, :d]
    for j in range(1, TOP_K):
        s = s + y[:, j * d:(j + 1) * d]
    x2 = x_ref[...] + gt2_ref[0] * s
    if final:
        x2 = _rms(x2) * gfin_ref[...]
    o_ref[...] = x2


def _combine_call(rows, n_tiles, x1, yg, mod3, gfin, final):
    d = x1.shape[1]
    return pl.pallas_call(
        functools.partial(_combine_kernel, final=final),
        grid=(n_tiles,),
        in_specs=[pl.BlockSpec((TM, d), lambda i: (i, 0)),
                  pl.BlockSpec((TM, TOP_K * d), lambda i: (i, 0)),
                  rows.mod_spec(5, d),
                  pl.BlockSpec((1, d), lambda i: (0, 0))],
        out_specs=pl.BlockSpec((TM, d), lambda i: (i, 0)),
        out_shape=jax.ShapeDtypeStruct((n_tiles * TM, d), F32),
        compiler_params=_cparams(("parallel",)),
        name="moe_combine_final" if final else "moe_combine",
    )(x1, yg, mod3, gfin)


def _route(top_idx, top_w):
    t = top_idx.shape[0]
    m = t * TOP_K
    flat_e = top_idx.reshape(m)
    onehot = (flat_e[:, None] == jnp.arange(N_EXPERTS, dtype=jnp.int32)[None, :]).astype(jnp.int32)
    csum = jnp.cumsum(onehot, axis=0)
    rank = jnp.sum(csum * onehot, axis=1) - 1
    counts = csum[-1]
    padded = ((counts + MOE_TB - 1) // MOE_TB) * MOE_TB
    pend = jnp.cumsum(padded)
    pstart = pend - padded
    dest = pstart[flat_e] + rank
    n_blk = -(-m // MOE_TB) + N_EXPERTS
    n_rows = n_blk * MOE_TB
    flat_t = jnp.repeat(jnp.arange(t, dtype=jnp.int32), TOP_K)
    buf_t = jnp.full((n_rows,), t, jnp.int32).at[dest].set(flat_t)
    buf_w = jnp.zeros((n_rows,), F32).at[dest].set(top_w.reshape(m))
    n_used = (pend[-1] // MOE_TB).astype(jnp.int32)
    blk_start = jnp.arange(n_blk, dtype=jnp.int32) * MOE_TB
    blk_e = jnp.clip(jnp.searchsorted(pend, blk_start, side="right"), 0, N_EXPERTS - 1).astype(jnp.int32)
    blk_e = jnp.where(jnp.arange(n_blk) < n_used, blk_e, blk_e[jnp.maximum(n_used - 1, 0)])
    return dest.reshape(t, TOP_K), buf_t, buf_w, blk_e, n_used.reshape(1)


def _rope_table(n, rot_dim, lane_off, period):
    qd = rot_dim // 4
    pos = jnp.arange(n, dtype=jnp.int32)
    rws = (pos // GRID_W).astype(F32)
    cls = (pos % GRID_W).astype(F32)
    inv = ROPE_BASE ** (-jnp.arange(qd, dtype=F32) / qd)
    lane = jnp.arange(LANES, dtype=jnp.int32)
    w = (lane - lane_off) % period
    active = (lane >= lane_off) & (w < rot_dim) if period == LANES else jnp.ones((LANES,), bool)
    slot = w // qd
    freq = inv[w % qd]
    ang = jnp.where((slot < 2)[None, :], rws[:, None], cls[:, None]) * freq[None, :]
    sign = jnp.where(slot % 2 == 0, -1.0, 1.0)
    cos = jnp.where(active[None, :], jnp.cos(ang), 1.0)
    sin = jnp.where(active[None, :], jnp.sin(ang) * sign[None, :], 0.0)
    ident = (jnp.ones((TM, LANES), F32), jnp.zeros((TM, LANES), F32))
    return jnp.concatenate([cos, ident[0]], 0), jnp.concatenate([sin, ident[1]], 0)


def _prep_weights(p):
    w_in = p["w_in"]
    nl, d, _ = w_in.shape
    o = {}
    o["w_a"] = w_in[:, :, :3 * A_WIDTH].astype(BF16)
    o["w_b"] = w_in[:, :, 3 * A_WIDTH:3 * A_WIDTH + 3 * B_WIDTH].astype(BF16)
    c0 = 3 * A_WIDTH + 3 * B_WIDTH
    c1 = c0 + C_Q_RANK + C_KV_RANK
    zeros = lambda k: jnp.zeros((nl, d, k), F32)
    o["w_c"] = jnp.concatenate([w_in[:, :, c0:c1], zeros(C_NOPE), w_in[:, :, c1:c1 + C_ROPE],
                                zeros(LANES - C_NOPE - C_ROPE)], axis=-1).astype(BF16)
    o["w_g"] = w_in[:, :, c1 + C_ROPE:].astype(BF16)
    wq = p["w_q_b"].reshape(nl, C_Q_RANK, C_HEADS, C_NOPE + C_ROPE)
    wq = jnp.pad(wq, ((0, 0), (0, 0), (0, 0), (0, LANES - C_NOPE - C_ROPE)))
    o["w_q_b"] = wq.reshape(nl, C_Q_RANK, C_HEADS * LANES).astype(BF16)
    wkv = p["w_kv_b"].reshape(nl, C_KV_RANK, C_HEADS, C_NOPE + C_VDIM)
    wk = jnp.pad(wkv[..., :C_NOPE], ((0, 0), (0, 0), (0, 0), (0, LANES - C_NOPE)))
    o["w_kv_b"] = jnp.concatenate([wk.reshape(nl, C_KV_RANK, C_HEADS * LANES),
                                   wkv[..., C_NOPE:].reshape(nl, C_KV_RANK, C_WIDTH)], axis=-1).astype(BF16)
    for name in ("w_br_a", "w_br_b", "w_br_c", "w_out", "w_down"):
        o[name] = p[name].astype(BF16)
    o["w_gate"] = p["w_gate_up"][..., 0::2].astype(BF16)
    o["w_up"] = p["w_gate_up"][..., 1::2].astype(BF16)
    o["b_gate"] = p["b_gate_up"][..., None, 0::2]
    o["b_up"] = p["b_gate_up"][..., None, 1::2]
    o["b_down"] = p["b_down"][..., None, :]
    o["w_router"] = jnp.pad(p["w_router"], ((0, 0), (0, 0), (0, LANES - N_EXPERTS)))
    o["b_router"] = jnp.pad(p["b_router"], ((0, 0), (0, LANES - N_EXPERTS)))[:, None, :]
    return o


@jax.jit
def _trunk(p):
    x, ctx = p["x"], p["ctx"]
    bsz, n, d = x.shape
    n_ctx = ctx.shape[1]
    depth = p["w_mod"].shape[0]
    rows = _Rows(bsz, n, n_ctx)
    w = _prep_weights(p)
    rope_a = _rope_table(n, A_HEAD_DIM, 0, A_HEAD_DIM)
    rope_c = _rope_table(n, C_ROPE, C_NOPE, LANES)
    scale_a = A_HEAD_DIM ** -0.5
    scale_b = B_HEAD_DIM ** -0.5
    scale_c = (C_NOPE + C_ROPE) ** -0.5
    c8 = jnp.zeros((8, d), F32).at[:bsz].set(p["c"]).at[bsz].set(p["c_ctx"])
    x_all = jnp.concatenate([x.reshape(bsz * n, d), ctx.reshape(bsz * n_ctx, d)], axis=0)
    dummy_lam = jnp.zeros((4, A_HEAD_DIM), F32)
    dummy_g = jnp.ones((1, LANES), F32)
    out = None
    for l in range(depth):
        last = l == depth - 1
        lam_init = 0.8 - 0.6 * math.exp(-0.3 * l)
        mod3 = _mod_call(c8, p["w_mod"][l], p["b_mod"][l]).reshape(8 * N_MOD, 1, d)
        g_mix = p["g_mix"][l][None, :]
        qa, ka, va = _inproj_qkv_call(rows, x_all, g_mix, mod3, w["w_a"][l], rope_a, scale_a)
        qb, kb, vb = _inproj_qkv_call(rows, x_all, g_mix, mod3, w["w_b"][l], None, scale_b)
        qc, kc, vc = _inproj_mla_call(rows, x_all, g_mix, mod3, w["w_c"][l], p["g_q_a"][l][None, :],
                                      w["w_q_b"][l], p["g_kv_a"][l][None, :], w["w_kv_b"][l], rope_c, scale_c)
        gates = _inproj_gates_call(rows, x_all, g_mix, mod3, w["w_g"][l])
        lamv = jnp.stack([p["lam_q1"][l], p["lam_k1"][l], p["lam_q2"][l], p["lam_k2"][l]])
        gs = p["g_subln"][l][None, :]
        attn = functools.partial(_pair_attention_call, rows)
        ya = attn(qa, ka, va, lamv, gs, wq=LANES, mode="diff", lam_init=lam_init, ctx_queries=False)
        yb = _nbr_attention_call(rows, qb, kb, vb, _nbr_bias_table(p["rpb"][l], n // GRID_W))
        yc = attn(qc, kc, vc, dummy_lam, dummy_g, wq=2 * LANES, mode="pair", lam_init=0.0, ctx_queries=False)
        if not last:
            ya_c = attn(qa, ka, va, lamv, gs, wq=LANES, mode="diff", lam_init=lam_init, ctx_queries=True)
            yb_c = attn(qb, kb, vb, dummy_lam, dummy_g, wq=LANES, mode="pair", lam_init=0.0, ctx_queries=True)
            yc_c = attn(qc, kc, vc, dummy_lam, dummy_g, wq=2 * LANES, mode="pair", lam_init=0.0, ctx_queries=True)
            ya, yb, yc = (jnp.concatenate(pair, axis=0) for pair in ((ya, ya_c), (yb, yb_c), (yc, yc_c)))
        n_tiles = rows.lat_tiles if last else rows.all_tiles
        x1, h2, ri, rw = _merge_call(rows, n_tiles, ya, yb, yc, gates, x_all, w["w_br_a"][l], w["w_br_b"][l],
                                     w["w_br_c"][l], w["w_out"][l], mod3, p["g_ffn"][l][None, :],
                                     w["w_router"][l], w["b_router"][l])
        dest, buf_t, buf_w, blk_e, n_used = _route(ri[:, :TOP_K], rw[:, :TOP_K])
        h2_pad = jnp.concatenate([h2, jnp.zeros((1, d), BF16)], axis=0)
        y = _moe_call(h2_pad[buf_t], buf_w[:, None], blk_e, n_used, w["w_gate"][l], w["b_gate"][l],
                      w["w_up"][l], w["b_up"][l], w["w_down"][l], w["b_down"][l])
        yg = y[dest].reshape(n_tiles * TM, TOP_K * d)
        x_new = _combine_call(rows, n_tiles, x1, yg, mod3, p["g_final"][None, :], last)
        if last:
            out = x_new.reshape(bsz, n, d)
        else:
            x_all = x_new
    return out


def kernel(x, c, ctx, c_ctx, w_mod, b_mod, g_mix, w_in, lam_q1, lam_k1, lam_q2, lam_k2, g_subln, rpb, g_q_a, w_q_b, g_kv_a, w_kv_b, w_br_a, w_br_b, w_br_c, w_out, g_ffn, w_router, b_router, w_gate_up, b_gate_up, w_down, b_down, g_final):
    return _trunk(dict(x=x, c=c, ctx=ctx, c_ctx=c_ctx, w_mod=w_mod, b_mod=b_mod, g_mix=g_mix, w_in=w_in,
                       lam_q1=lam_q1, lam_k1=lam_k1, lam_q2=lam_q2, lam_k2=lam_k2, g_subln=g_subln, rpb=rpb,
                       g_q_a=g_q_a, w_q_b=w_q_b, g_kv_a=g_kv_a, w_kv_b=w_kv_b, w_br_a=w_br_a, w_br_b=w_br_b,
                       w_br_c=w_br_c, w_out=w_out, g_ffn=g_ffn, w_router=w_router, b_router=b_router,
                       w_gate_up=w_gate_up, b_gate_up=b_gate_up, w_down=w_down, b_down=b_down, g_final=g_final))
```

```python
import functools
import math

import jax
import jax.numpy as jnp
from jax import lax
from jax.experimental import pallas as pl
from jax.experimental.pallas import tpu as pltpu

GRID_W = 64
A_HEADS = 4
A_HEAD_DIM = 64
A_WIDTH = A_HEADS * 2 * A_HEAD_DIM
B_HEADS = 8
B_HEAD_DIM = 64
B_WIDTH = B_HEADS * B_HEAD_DIM
NA_ROWS = 8
NA_COLS = 16
C_HEADS = 8
C_NOPE = 64
C_ROPE = 32
C_VDIM = 64
C_Q_RANK = 384
C_KV_RANK = 256
C_WIDTH = C_HEADS * C_VDIM
N_BRANCH = 3
N_EXPERTS = 32
TOP_K = 4
SWIGLU_LIMIT = 7.0
SWIGLU_ALPHA = 1.702
N_MOD = 6
ROPE_BASE = 10000.0
EPS = 1e-6
NEG_INF = -1e30
LOG2E = 1.4426950408889634

LANES = 128
VMEM_LIMIT = 56 * 1024 * 1024
TM = 256
TQ = 256
TK = 2048
MOE_TB = 256
NB_ROWS_PER_STEP = 8

BF16 = jnp.bfloat16
F32 = jnp.float32


def _cparams(sem):
    return pltpu.CompilerParams(dimension_semantics=sem, vmem_limit_bytes=VMEM_LIMIT)


def _dot(a, b):
    return jnp.dot(a, b, preferred_element_type=F32)


def _dot_nt(a, b):
    return lax.dot_general(a, b, (((1,), (1,)), ((), ())), preferred_element_type=F32)


def _split_bf16(a):
    hi = a.astype(BF16)
    lo = (a - hi.astype(F32)).astype(BF16)
    return hi, lo


def _dot_split(a, b):
    a_hi, a_lo = _split_bf16(a)
    b_hi, b_lo = _split_bf16(b)
    return _dot(a_hi, b_hi) + _dot(a_hi, b_lo) + _dot(a_lo, b_hi)


def _rms(x):
    return x * lax.rsqrt(jnp.mean(x * x, axis=-1, keepdims=True) + EPS)


def _modulated(x_ref, g_ref, sh_ref, sc_ref):
    return (_rms(x_ref[...]) * g_ref[...]) * (1.0 + sc_ref[0]) + sh_ref[0]


def _rope(blk, cos, sin_signed, shift):
    lane = lax.broadcasted_iota(jnp.int32, blk.shape, 1)
    first = ((lane // shift) % 2) == 0
    partner = jnp.where(first, pltpu.roll(blk, LANES - shift, 1), pltpu.roll(blk, shift, 1))
    return blk * cos + partner * sin_signed


def _mod_kernel(c_ref, w_ref, b_ref, o_ref):
    cc = c_ref[...]
    a = cc * jax.nn.sigmoid(cc)
    o_ref[...] = _dot_split(a, w_ref[...]) + b_ref[...]


def _mod_call(c8, w_mod, b_mod):
    d = c8.shape[1]
    n_out = w_mod.shape[1]
    return pl.pallas_call(
        _mod_kernel,
        grid=(n_out // d,),
        in_specs=[pl.BlockSpec((8, d), lambda j: (0, 0)),
                  pl.BlockSpec((d, d), lambda j: (0, j)),
                  pl.BlockSpec((1, d), lambda j: (0, j))],
        out_specs=pl.BlockSpec((8, d), lambda j: (0, j)),
        out_shape=jax.ShapeDtypeStruct((8, n_out), F32),
        compiler_params=_cparams(("arbitrary",)),
        name="mod",
    )(c8, w_mod, b_mod.reshape(1, n_out))


class _Rows:
    def __init__(self, bsz, n, n_ctx):
        assert n % TM == 0 and (bsz * n_ctx) % TM == 0
        self.bsz, self.n, self.n_ctx = bsz, n, n_ctx
        self.t_lat = bsz * n
        self.t_all = bsz * n + bsz * n_ctx
        self.tiles_per_batch = n // TM
        self.lat_tiles = self.t_lat // TM
        self.all_tiles = self.t_all // TM

    def mod_spec(self, which, d):
        tpb, bsz = self.tiles_per_batch, self.bsz
        return pl.BlockSpec((1, 1, d), lambda i, *_: (jnp.minimum(i // tpb, bsz) * N_MOD + which, 0, 0))

    def rope_spec(self):
        tpb, lat = self.tiles_per_batch, self.lat_tiles
        return pl.BlockSpec((TM, LANES), lambda i, *_: (jnp.where(i < lat, i % tpb, tpb), 0))


def _inproj_qkv_kernel(*refs, rope, qscale):
    if rope:
        x_ref, g_ref, sh_ref, sc_ref, w_ref, cos_ref, sin_ref, q_ref, k_ref, v_ref = refs
    else:
        x_ref, g_ref, sh_ref, sc_ref, w_ref, q_ref, k_ref, v_ref = refs
    h = _modulated(x_ref, g_ref, sh_ref, sc_ref).astype(BF16)
    t = _dot(h, w_ref[...])
    outs = (q_ref, k_ref, v_ref)
    per = q_ref.shape[1] // LANES
    for j in range(3 * per):
        blk = t[:, j * LANES:(j + 1) * LANES]
        if rope and j < 2 * per:
            blk = _rope(blk, cos_ref[...], sin_ref[...], A_HEAD_DIM // 4)
        if j < per:
            blk = blk * qscale
        outs[j // per][:, (j % per) * LANES:(j % per + 1) * LANES] = blk.astype(BF16)


def _inproj_qkv_call(rows, x_all, g, mod3, w, tables, qscale):
    d = x_all.shape[1]
    width = w.shape[1] // 3
    rope = tables is not None
    in_specs = [pl.BlockSpec((TM, d), lambda i: (i, 0)),
                pl.BlockSpec((1, d), lambda i: (0, 0)),
                rows.mod_spec(0, d), rows.mod_spec(1, d),
                pl.BlockSpec((d, 3 * width), lambda i: (0, 0))]
    args = [x_all, g, mod3, mod3, w]
    if rope:
        in_specs += [rows.rope_spec(), rows.rope_spec()]
        args += list(tables)
    out = jax.ShapeDtypeStruct((rows.t_all, width), BF16)
    return pl.pallas_call(
        functools.partial(_inproj_qkv_kernel, rope=rope, qscale=qscale),
        grid=(rows.all_tiles,),
        in_specs=in_specs,
        out_specs=[pl.BlockSpec((TM, width), lambda i: (i, 0))] * 3,
        out_shape=[out] * 3,
        compiler_params=_cparams(("parallel",)),
        name="inproj_qkv_rope" if rope else "inproj_qkv",
    )(*args)


def _inproj_gates_kernel(x_ref, g_ref, sh_ref, sc_ref, w_ref, o_ref, h_scr):
    @pl.when(pl.program_id(1) == 0)
    def _():
        h_scr[...] = _modulated(x_ref, g_ref, sh_ref, sc_ref).astype(BF16)

    o_ref[...] = jax.nn.sigmoid(_dot(h_scr[...], w_ref[...])).astype(BF16)


def _inproj_gates_call(rows, x_all, g, mod3, w):
    d = x_all.shape[1]
    n_out = w.shape[1]
    return pl.pallas_call(
        _inproj_gates_kernel,
        grid=(rows.all_tiles, n_out // d),
        in_specs=[pl.BlockSpec((TM, d), lambda i, j: (i, 0)),
                  pl.BlockSpec((1, d), lambda i, j: (0, 0)),
                  rows.mod_spec(0, d), rows.mod_spec(1, d),
                  pl.BlockSpec((d, d), lambda i, j: (0, j))],
        out_specs=pl.BlockSpec((TM, d), lambda i, j: (i, j)),
        out_shape=jax.ShapeDtypeStruct((rows.t_all, n_out), BF16),
        scratch_shapes=[pltpu.VMEM((TM, d), BF16)],
        compiler_params=_cparams(("parallel", "arbitrary")),
        name="inproj_gates",
    )(x_all, g, mod3, mod3, w)


def _inproj_mla_kernel(x_ref, g_ref, sh_ref, sc_ref, w_ref, gq_ref, wq_ref, gkv_ref, wkv_ref,
                       cos_ref, sin_ref, q_ref, k_ref, v_ref, *, qscale):
    h = _modulated(x_ref, g_ref, sh_ref, sc_ref).astype(BF16)
    t = _dot(h, w_ref[...])
    cos, sin = cos_ref[...], sin_ref[...]
    shift = C_ROPE // 4
    cq = (_rms(t[:, :C_Q_RANK]) * gq_ref[...]).astype(BF16)
    q = _dot(cq, wq_ref[...])
    for j in range(C_HEADS):
        blk = _rope(q[:, j * LANES:(j + 1) * LANES], cos, sin, shift) * qscale
        q_ref[:, j * LANES:(j + 1) * LANES] = blk.astype(BF16)
    ckv = (_rms(t[:, C_Q_RANK:C_Q_RANK + C_KV_RANK]) * gkv_ref[...]).astype(BF16)
    kv = _dot(ckv, wkv_ref[...])
    pe = _rope(t[:, C_Q_RANK + C_KV_RANK:], cos, sin, shift)
    for j in range(C_HEADS):
        k_ref[:, j * LANES:(j + 1) * LANES] = (kv[:, j * LANES:(j + 1) * LANES] + pe).astype(BF16)
    v_ref[...] = kv[:, C_HEADS * LANES:].astype(BF16)


def _inproj_mla_call(rows, x_all, g, mod3, w, gq, wq, gkv, wkv, tables, qscale):
    d = x_all.shape[1]
    full = lambda a: pl.BlockSpec(a.shape, lambda i: (0,) * a.ndim)
    qk = jax.ShapeDtypeStruct((rows.t_all, C_HEADS * LANES), BF16)
    return pl.pallas_call(
        functools.partial(_inproj_mla_kernel, qscale=qscale),
        grid=(rows.all_tiles,),
        in_specs=[pl.BlockSpec((TM, d), lambda i: (i, 0)), full(g),
                  rows.mod_spec(0, d), rows.mod_spec(1, d),
                  full(w), full(gq), full(wq), full(gkv), full(wkv),
                  rows.rope_spec(), rows.rope_spec()],
        out_specs=[pl.BlockSpec((TM, C_HEADS * LANES), lambda i: (i, 0)),
                   pl.BlockSpec((TM, C_HEADS * LANES), lambda i: (i, 0)),
                   pl.BlockSpec((TM, C_WIDTH), lambda i: (i, 0))],
        out_shape=[qk, qk, jax.ShapeDtypeStruct((rows.t_all, C_WIDTH), BF16)],
        compiler_params=_cparams(("parallel",)),
        name="inproj_mla",
    )(x_all, g, mod3, mod3, w, gq, wq, gkv, wkv, *tables)


def _pair_attention_kernel(*refs, n_lat_chunks, tk, mode, lam_init):
    if n_lat_chunks:
        q_ref, kl_ref, vl_ref, kc_ref, vc_ref, lam_ref, gs_ref, o_ref = refs
    else:
        q_ref, kc_ref, vc_ref, lam_ref, gs_ref, o_ref = refs
    q = q_ref[...]
    tq, wq = q.shape
    lane = lax.broadcasted_iota(jnp.int32, q.shape, 1)
    lower = lane < wq // 2
    zero = jnp.zeros_like(q)
    qs = jnp.concatenate([jnp.where(lower, q, zero), jnp.where(lower, zero, q)], axis=0)

    def step(kc, vc, carry):
        m, l, acc = carry
        s = _dot_nt(qs, kc)
        m_new = jnp.maximum(m, jnp.max(s, axis=-1, keepdims=True))
        p = jnp.exp2(s - m_new)
        alpha = jnp.exp2(m - m_new)
        l = alpha * l + jnp.sum(p, axis=-1, keepdims=True)
        acc = alpha * acc + _dot(p.astype(BF16), vc)
        return m_new, l, acc

    carry = (jnp.full((2 * tq, 1), NEG_INF, F32), jnp.zeros((2 * tq, 1), F32),
             jnp.zeros((2 * tq, LANES), F32))
    for i in range(n_lat_chunks):
        carry = step(kl_ref[i * tk:(i + 1) * tk, :], vl_ref[i * tk:(i + 1) * tk, :], carry)
    m, l, acc = step(kc_ref[...], vc_ref[...], carry)
    o = acc / l
    o0, o1 = o[:tq], o[tq:]
    if mode == "diff":
        lv = lam_ref[...]
        lam = (jnp.exp(jnp.sum(lv[0:1] * lv[1:2], axis=-1, keepdims=True))
               - jnp.exp(jnp.sum(lv[2:3] * lv[3:4], axis=-1, keepdims=True)) + lam_init)
        out = (_rms(o0 - lam * o1) * gs_ref[...]) * (1.0 - lam_init)
    else:
        out = jnp.where(lax.broadcasted_iota(jnp.int32, o0.shape, 1) < LANES // 2, o0, o1)
    o_ref[...] = out.astype(BF16)


def _pair_attention_call(rows, q, k, v, lamv, gs, *, wq, mode, lam_init, ctx_queries):
    bsz, n, n_ctx = rows.bsz, rows.n, rows.n_ctx
    n_pairs = v.shape[1] // LANES
    ctx_blk0 = rows.t_lat // n_ctx
    kc_spec = pl.BlockSpec((n_ctx, wq), lambda b, h, i: (ctx_blk0 + b, h))
    vc_spec = pl.BlockSpec((n_ctx, LANES), lambda b, h, i: (ctx_blk0 + b, h))
    small = [pl.BlockSpec(lamv.shape, lambda b, h, i: (0, 0)), pl.BlockSpec(gs.shape, lambda b, h, i: (0, 0))]
    if ctx_queries:
        tq, n_q, n_chunks, tk = n_ctx, 1, 0, 0
        q_spec = pl.BlockSpec((tq, wq), lambda b, h, i: (ctx_blk0 + b, h))
        o_spec = pl.BlockSpec((tq, LANES), lambda b, h, i: (b, h))
        in_specs = [q_spec, kc_spec, vc_spec] + small
        args = (q, k, v, lamv, gs)
        out_rows = bsz * n_ctx
    else:
        tk = min(TK, n)
        assert n % TQ == 0 and n % tk == 0
        tq, n_q, n_chunks = TQ, n // TQ, n // tk
        q_spec = pl.BlockSpec((tq, wq), lambda b, h, i: (b * n_q + i, h))
        o_spec = pl.BlockSpec((tq, LANES), lambda b, h, i: (b * n_q + i, h))
        in_specs = [q_spec,
                    pl.BlockSpec((n, wq), lambda b, h, i: (b, h)),
                    pl.BlockSpec((n, LANES), lambda b, h, i: (b, h)),
                    kc_spec, vc_spec] + small
        args = (q, k, v, k, v, lamv, gs)
        out_rows = rows.t_lat
    return pl.pallas_call(
        functools.partial(_pair_attention_kernel, n_lat_chunks=n_chunks, tk=tk, mode=mode, lam_init=lam_init),
        grid=(bsz, n_pairs, n_q),
        in_specs=in_specs,
        out_specs=o_spec,
        out_shape=jax.ShapeDtypeStruct((out_rows, n_pairs * LANES), BF16),
        compiler_params=_cparams(("parallel", "parallel", "arbitrary")),
        name=f"attn_{mode}_{'ctx' if ctx_queries else 'lat'}_{wq}",
    )(*args)


def _nbr_classes(n_rows):
    half = NA_ROWS // 2
    reps = list(range(half)) + [half] + list(range(n_rows - half + 1, n_rows))
    return reps


def _nbr_bias_table(rpb, n_rows):
    c_idx = jnp.arange(GRID_W, dtype=jnp.int32)
    col_start = jnp.clip(c_idx - NA_COLS // 2, 0, GRID_W - NA_COLS)
    col_mask = (c_idx[None, :] >= col_start[:, None]) & (c_idx[None, :] < col_start[:, None] + NA_COLS)
    col_bias_idx = jnp.clip(c_idx[None, :] - c_idx[:, None] + NA_COLS - 1, 0, 2 * NA_COLS - 2)
    rpb_c = rpb[:, :, col_bias_idx]
    tabs = []
    for r in _nbr_classes(n_rows):
        rs = min(max(r - NA_ROWS // 2, 0), n_rows - NA_ROWS)
        di = rs + jnp.arange(NA_ROWS) - r + NA_ROWS - 1
        bias = jnp.transpose(rpb_c[:, di], (0, 2, 1, 3))
        bias = jnp.where(col_mask[None, :, None, :], bias * LOG2E, NEG_INF)
        tabs.append(bias.reshape(rpb.shape[0], GRID_W, NA_ROWS * GRID_W))
    return jnp.stack(tabs).astype(F32)


def _nbr_attention_kernel(q_ref, kl_ref, vl_ref, kc_ref, vc_ref, bias_ref, o_ref, *, n_rows):
    half = NA_ROWS // 2
    kc, vc = kc_ref[...], vc_ref[...]
    win = NA_ROWS * GRID_W

    def one_row(j, _):
        r = pl.program_id(2) * NB_ROWS_PER_STEP + j
        rs = jnp.clip(r - half, 0, n_rows - NA_ROWS)
        cls = jnp.minimum(r, half) + jnp.maximum(r - (n_rows - half), 0)
        q = q_ref[pl.ds(pl.multiple_of(j * GRID_W, GRID_W), GRID_W), :]
        lane = lax.broadcasted_iota(jnp.int32, q.shape, 1)
        lower = lane < LANES // 2
        zero = jnp.zeros_like(q)
        qs = jnp.concatenate([jnp.where(lower, q, zero), jnp.where(lower, zero, q)], axis=0)
        off = pl.multiple_of(rs * GRID_W, GRID_W)
        kw = kl_ref[pl.ds(off, win), :]
        vw = vl_ref[pl.ds(off, win), :]
        bias = bias_ref[cls].reshape(2 * GRID_W, win)
        s = jnp.where(bias > 0.5 * NEG_INF, _dot_nt(qs, kw) + bias, NEG_INF)
        s_ctx = _dot_nt(qs, kc)
        m = jnp.maximum(jnp.max(s, axis=-1, keepdims=True), jnp.max(s_ctx, axis=-1, keepdims=True))
        p = jnp.exp2(s - m)
        p_ctx = jnp.exp2(s_ctx - m)
        l = jnp.sum(p, axis=-1, keepdims=True) + jnp.sum(p_ctx, axis=-1, keepdims=True)
        o = (_dot(p.astype(BF16), vw) + _dot(p_ctx.astype(BF16), vc)) / l
        out = jnp.where(lax.broadcasted_iota(jnp.int32, (GRID_W, LANES), 1) < LANES // 2,
                        o[:GRID_W], o[GRID_W:])
        o_ref[pl.ds(pl.multiple_of(j * GRID_W, GRID_W), GRID_W), :] = out.astype(BF16)
        return 0

    lax.fori_loop(0, NB_ROWS_PER_STEP, one_row, 0, unroll=4)


def _nbr_attention_call(rows, q, k, v, bias):
    bsz, n, n_ctx = rows.bsz, rows.n, rows.n_ctx
    n_rows = n // GRID_W
    assert n_rows % NB_ROWS_PER_STEP == 0 and n_rows >= 2 * NA_ROWS
    steps = n_rows // NB_ROWS_PER_STEP
    n_pairs = B_HEADS // 2
    ctx_blk0 = rows.t_lat // n_ctx
    tq = NB_ROWS_PER_STEP * GRID_W
    n_cls = bias.shape[0]
    return pl.pallas_call(
        functools.partial(_nbr_attention_kernel, n_rows=n_rows),
        grid=(bsz, n_pairs, steps),
        in_specs=[pl.BlockSpec((tq, LANES), lambda b, h, i: (b * steps + i, h)),
                  pl.BlockSpec((n, LANES), lambda b, h, i: (b, h)),
                  pl.BlockSpec((n, LANES), lambda b, h, i: (b, h)),
                  pl.BlockSpec((n_ctx, LANES), lambda b, h, i: (ctx_blk0 + b, h)),
                  pl.BlockSpec((n_ctx, LANES), lambda b, h, i: (ctx_blk0 + b, h)),
                  pl.BlockSpec((n_cls, 2, GRID_W, NA_ROWS * GRID_W), lambda b, h, i: (0, h, 0, 0))],
        out_specs=pl.BlockSpec((tq, LANES), lambda b, h, i: (b * steps + i, h)),
        out_shape=jax.ShapeDtypeStruct((rows.t_lat, B_WIDTH), BF16),
        compiler_params=_cparams(("parallel", "parallel", "arbitrary")),
        name="attn_nbr",
    )(q, k, v, k, v, bias)


def _merge_kernel(ya_ref, yb_ref, yc_ref, g_ref, x_ref, wa_ref, wb_ref, wc_ref, wo_ref,
                  gt1_ref, gffn_ref, sh2_ref, sc2_ref, wr_ref, br_ref,
                  x1_ref, h2_ref, ri_ref, rw_ref):
    d = x_ref.shape[1]
    g = g_ref[...].astype(F32)
    m = (g[:, :d] * _dot(ya_ref[...], wa_ref[...])
         + g[:, d:2 * d] * _dot(yb_ref[...], wb_ref[...])
         + g[:, 2 * d:] * _dot(yc_ref[...], wc_ref[...]))
    y = _dot(m.astype(BF16), wo_ref[...])
    x1 = x_ref[...] + gt1_ref[0] * y
    x1_ref[...] = x1
    h2 = (_rms(x1) * gffn_ref[...]) * (1.0 + sc2_ref[0]) + sh2_ref[0]
    h2_ref[...] = h2.astype(BF16)
    logits = _dot_split(h2, wr_ref[...]) + br_ref[...]
    lane = lax.broadcasted_iota(jnp.int32, logits.shape, 1)
    lane_f = lane.astype(F32)
    work = jnp.where(lane < N_EXPERTS, logits, NEG_INF)
    idx_out = jnp.zeros(logits.shape, F32)
    val_out = jnp.zeros(logits.shape, F32)
    top = None
    den = jnp.zeros((logits.shape[0], 1), F32)
    for j in range(TOP_K):
        mx = jnp.max(work, axis=-1, keepdims=True)
        idx = jnp.min(jnp.where(work == mx, lane_f, float(LANES)), axis=-1, keepdims=True)
        if top is None:
            top = mx
        e = jnp.exp(mx - top)
        den = den + e
        idx_out = jnp.where(lane == j, idx, idx_out)
        val_out = jnp.where(lane == j, e, val_out)
        work = jnp.where(lane_f == idx, NEG_INF, work)
    ri_ref[...] = idx_out.astype(jnp.int32)
    rw_ref[...] = val_out / den


def _merge_call(rows, n_tiles, ya, yb, yc, gates, x_all, wa, wb, wc, wo, mod3, gffn, wr, br):
    d = x_all.shape[1]
    t = n_tiles * TM
    full = lambda a: pl.BlockSpec(a.shape, lambda i: (0,) * a.ndim)
    row = lambda w: pl.BlockSpec((TM, w), lambda i: (i, 0))
    return pl.pallas_call(
        _merge_kernel,
        grid=(n_tiles,),
        in_specs=[row(ya.shape[1]), row(yb.shape[1]), row(yc.shape[1]), row(gates.shape[1]), row(d),
                  full(wa), full(wb), full(wc), full(wo),
                  rows.mod_spec(2, d), full(gffn), rows.mod_spec(3, d), rows.mod_spec(4, d),
                  full(wr), full(br)],
        out_specs=[row(d), row(d), row(LANES), row(LANES)],
        out_shape=[jax.ShapeDtypeStruct((t, d), F32), jax.ShapeDtypeStruct((t, d), BF16),
                   jax.ShapeDtypeStruct((t, LANES), jnp.int32), jax.ShapeDtypeStruct((t, LANES), F32)],
        compiler_params=_cparams(("parallel",)),
        name="merge_router",
    )(ya, yb, yc, gates, x_all, wa, wb, wc, wo, mod3, gffn, mod3, mod3, wr, br)


def _moe_kernel(blk_e_ref, n_used_ref, x_ref, wgu_ref, bgu_ref, wd_ref, bd_ref, o_ref):
    i = pl.program_id(0)

    @pl.when(i < n_used_ref[0])
    def _():
        gu = _dot(x_ref[...], wgu_ref[0]) + bgu_ref[0]
        gate = jnp.minimum(gu, SWIGLU_LIMIT)
        glu = gate * jax.nn.sigmoid(SWIGLU_ALPHA * gate)
        up1 = jnp.clip(gu, -SWIGLU_LIMIT, SWIGLU_LIMIT) + 1.0
        up1 = jnp.concatenate([pltpu.roll(up1[:, j * LANES:(j + 1) * LANES], LANES - 1, 1)
                               for j in range(gu.shape[1] // LANES)], axis=1)
        act = (glu * up1).astype(BF16)
        o_ref[...] = _dot(act, wd_ref[0]) + bd_ref[0]

    @pl.when(i >= n_used_ref[0])
    def _():
        o_ref[...] = jnp.zeros_like(o_ref)


def _moe_call(xg, blk_e, n_used, wgu, bgu, wd, bd):
    n_rows, d = xg.shape
    f2 = wgu.shape[2]
    n_blk = n_rows // MOE_TB
    last = lambda i, be, nu: jnp.minimum(i, nu[0] - 1)
    grid_spec = pltpu.PrefetchScalarGridSpec(
        num_scalar_prefetch=2,
        grid=(n_blk,),
        in_specs=[pl.BlockSpec((MOE_TB, d), lambda i, be, nu: (last(i, be, nu), 0)),
                  pl.BlockSpec((1, d, f2), lambda i, be, nu: (be[i], 0, 0)),
                  pl.BlockSpec((1, 1, f2), lambda i, be, nu: (be[i], 0, 0)),
                  pl.BlockSpec((1, f2, d), lambda i, be, nu: (be[i], 0, 0)),
                  pl.BlockSpec((1, 1, d), lambda i, be, nu: (be[i], 0, 0))],
        out_specs=pl.BlockSpec((MOE_TB, d), lambda i, be, nu: (i, 0)),
    )
    return pl.pallas_call(
        _moe_kernel,
        grid_spec=grid_spec,
        out_shape=jax.ShapeDtypeStruct((n_rows, d), F32),
        compiler_params=_cparams(("arbitrary",)),
        name="moe_experts",
    )(blk_e, n_used, xg, wgu, bgu, wd, bd)


def _combine_kernel(x_ref, y_ref, rw_ref, gt2_ref, gfin_ref, o_ref, *, final):
    d = x_ref.shape[1]
    y = y_ref[...]
    rw = rw_ref[...]
    s = y[:, :d] * rw[:, 0:1]
    for j in range(1, TOP_K):
        s = s + y[:, j * d:(j + 1) * d] * rw[:, j:j + 1]
    x2 = x_ref[...] + gt2_ref[0] * s
    if final:
        x2 = _rms(x2) * gfin_ref[...]
    o_ref[...] = x2


def _combine_call(rows, n_tiles, x1, yg, rw, mod3, gfin, final):
    d = x1.shape[1]
    return pl.pallas_call(
        functools.partial(_combine_kernel, final=final),
        grid=(n_tiles,),
        in_specs=[pl.BlockSpec((TM, d), lambda i: (i, 0)),
                  pl.BlockSpec((TM, TOP_K * d), lambda i: (i, 0)),
                  pl.BlockSpec((TM, LANES), lambda i: (i, 0)),
                  rows.mod_spec(5, d),
                  pl.BlockSpec((1, d), lambda i: (0, 0))],
        out_specs=pl.BlockSpec((TM, d), lambda i: (i, 0)),
        out_shape=jax.ShapeDtypeStruct((n_tiles * TM, d), F32),
        compiler_params=_cparams(("parallel",)),
        name="moe_combine_final" if final else "moe_combine",
    )(x1, yg, rw, mod3, gfin)


def _route(top_idx):
    t = top_idx.shape[0]
    m = t * TOP_K
    flat_e = top_idx.reshape(m)
    onehot = (flat_e[:, None] == jnp.arange(N_EXPERTS, dtype=jnp.int32)[None, :]).astype(jnp.int32)
    csum = jnp.cumsum(onehot, axis=0)
    rank = jnp.sum(csum * onehot, axis=1) - 1
    counts = csum[-1]
    padded = ((counts + MOE_TB - 1) // MOE_TB) * MOE_TB
    pend = jnp.cumsum(padded)
    pstart = pend - padded
    dest = pstart[flat_e] + rank
    n_blk = -(-m // MOE_TB) + N_EXPERTS
    n_rows = n_blk * MOE_TB
    flat_t = jnp.repeat(jnp.arange(t, dtype=jnp.int32), TOP_K)
    buf_t = jnp.full((n_rows,), t, jnp.int32).at[dest].set(flat_t)
    n_used = (pend[-1] // MOE_TB).astype(jnp.int32)
    blk_start = jnp.arange(n_blk, dtype=jnp.int32) * MOE_TB
    blk_e = jnp.clip(jnp.searchsorted(pend, blk_start, side="right"), 0, N_EXPERTS - 1).astype(jnp.int32)
    blk_e = jnp.where(jnp.arange(n_blk) < n_used, blk_e, blk_e[jnp.maximum(n_used - 1, 0)])
    return dest.reshape(t, TOP_K), buf_t, blk_e, n_used.reshape(1)


def _rope_table(n, rot_dim, lane_off, period):
    qd = rot_dim // 4
    pos = jnp.arange(n, dtype=jnp.int32)
    rws = (pos // GRID_W).astype(F32)
    cls = (pos % GRID_W).astype(F32)
    inv = ROPE_BASE ** (-jnp.arange(qd, dtype=F32) / qd)
    lane = jnp.arange(LANES, dtype=jnp.int32)
    w = (lane - lane_off) % period
    active = (lane >= lane_off) & (w < rot_dim) if period == LANES else jnp.ones((LANES,), bool)
    slot = w // qd
    freq = inv[w % qd]
    ang = jnp.where((slot < 2)[None, :], rws[:, None], cls[:, None]) * freq[None, :]
    sign = jnp.where(slot % 2 == 0, -1.0, 1.0)
    cos = jnp.where(active[None, :], jnp.cos(ang), 1.0)
    sin = jnp.where(active[None, :], jnp.sin(ang) * sign[None, :], 0.0)
    ident = (jnp.ones((TM, LANES), F32), jnp.zeros((TM, LANES), F32))
    return jnp.concatenate([cos, ident[0]], 0), jnp.concatenate([sin, ident[1]], 0)


def _prep_weights(p):
    w_in = p["w_in"]
    nl, d, _ = w_in.shape
    o = {}
    o["w_a"] = w_in[:, :, :3 * A_WIDTH].astype(BF16)
    o["w_b"] = w_in[:, :, 3 * A_WIDTH:3 * A_WIDTH + 3 * B_WIDTH].astype(BF16)
    c0 = 3 * A_WIDTH + 3 * B_WIDTH
    c1 = c0 + C_Q_RANK + C_KV_RANK
    zeros = lambda k: jnp.zeros((nl, d, k), F32)
    o["w_c"] = jnp.concatenate([w_in[:, :, c0:c1], zeros(C_NOPE), w_in[:, :, c1:c1 + C_ROPE],
                                zeros(LANES - C_NOPE - C_ROPE)], axis=-1).astype(BF16)
    o["w_g"] = w_in[:, :, c1 + C_ROPE:].astype(BF16)
    wq = p["w_q_b"].reshape(nl, C_Q_RANK, C_HEADS, C_NOPE + C_ROPE)
    wq = jnp.pad(wq, ((0, 0), (0, 0), (0, 0), (0, LANES - C_NOPE - C_ROPE)))
    o["w_q_b"] = wq.reshape(nl, C_Q_RANK, C_HEADS * LANES).astype(BF16)
    wkv = p["w_kv_b"].reshape(nl, C_KV_RANK, C_HEADS, C_NOPE + C_VDIM)
    wk = jnp.pad(wkv[..., :C_NOPE], ((0, 0), (0, 0), (0, 0), (0, LANES - C_NOPE)))
    o["w_kv_b"] = jnp.concatenate([wk.reshape(nl, C_KV_RANK, C_HEADS * LANES),
                                   wkv[..., C_NOPE:].reshape(nl, C_KV_RANK, C_WIDTH)], axis=-1).astype(BF16)
    for name in ("w_br_a", "w_br_b", "w_br_c", "w_out", "w_gate_up"):
        o[name] = p[name].astype(BF16)
    wd = p["w_down"].astype(BF16)
    o["w_down"] = jnp.stack([wd, jnp.zeros_like(wd)], axis=3).reshape(wd.shape[0], wd.shape[1], -1, wd.shape[3])
    o["b_gate_up"] = p["b_gate_up"][..., None, :]
    o["b_down"] = p["b_down"][..., None, :]
    o["w_router"] = jnp.pad(p["w_router"], ((0, 0), (0, 0), (0, LANES - N_EXPERTS)))
    o["b_router"] = jnp.pad(p["b_router"], ((0, 0), (0, LANES - N_EXPERTS)))[:, None, :]
    return o


@jax.jit
def _trunk(p):
    x, ctx = p["x"], p["ctx"]
    bsz, n, d = x.shape
    n_ctx = ctx.shape[1]
    depth = p["w_mod"].shape[0]
    rows = _Rows(bsz, n, n_ctx)
    w = _prep_weights(p)
    rope_a = _rope_table(n, A_HEAD_DIM, 0, A_HEAD_DIM)
    rope_c = _rope_table(n, C_ROPE, C_NOPE, LANES)
    scale_a = A_HEAD_DIM ** -0.5 * LOG2E
    scale_b = B_HEAD_DIM ** -0.5 * LOG2E
    scale_c = (C_NOPE + C_ROPE) ** -0.5 * LOG2E
    c8 = jnp.zeros((8, d), F32).at[:bsz].set(p["c"]).at[bsz].set(p["c_ctx"])
    x_all = jnp.concatenate([x.reshape(bsz * n, d), ctx.reshape(bsz * n_ctx, d)], axis=0)
    dummy_lam = jnp.zeros((4, A_HEAD_DIM), F32)
    dummy_g = jnp.ones((1, LANES), F32)
    out = None
    for l in range(depth):
        last = l == depth - 1
        lam_init = 0.8 - 0.6 * math.exp(-0.3 * l)
        mod3 = _mod_call(c8, p["w_mod"][l], p["b_mod"][l]).reshape(8 * N_MOD, 1, d)
        g_mix = p["g_mix"][l][None, :]
        qa, ka, va = _inproj_qkv_call(rows, x_all, g_mix, mod3, w["w_a"][l], rope_a, scale_a)
        qb, kb, vb = _inproj_qkv_call(rows, x_all, g_mix, mod3, w["w_b"][l], None, scale_b)
        qc, kc, vc = _inproj_mla_call(rows, x_all, g_mix, mod3, w["w_c"][l], p["g_q_a"][l][None, :],
                                      w["w_q_b"][l], p["g_kv_a"][l][None, :], w["w_kv_b"][l], rope_c, scale_c)
        gates = _inproj_gates_call(rows, x_all, g_mix, mod3, w["w_g"][l])
        lamv = jnp.stack([p["lam_q1"][l], p["lam_k1"][l], p["lam_q2"][l], p["lam_k2"][l]])
        gs = p["g_subln"][l][None, :]
        attn = functools.partial(_pair_attention_call, rows)
        ya = attn(qa, ka, va, lamv, gs, wq=LANES, mode="diff", lam_init=lam_init, ctx_queries=False)
        yb = _nbr_attention_call(rows, qb, kb, vb, _nbr_bias_table(p["rpb"][l], n // GRID_W))
        yc = attn(qc, kc, vc, dummy_lam, dummy_g, wq=2 * LANES, mode="pair", lam_init=0.0, ctx_queries=False)
        if not last:
            ya_c = attn(qa, ka, va, lamv, gs, wq=LANES, mode="diff", lam_init=lam_init, ctx_queries=True)
            yb_c = attn(qb, kb, vb, dummy_lam, dummy_g, wq=LANES, mode="pair", lam_init=0.0, ctx_queries=True)
            yc_c = attn(qc, kc, vc, dummy_lam, dummy_g, wq=2 * LANES, mode="pair", lam_init=0.0, ctx_queries=True)
            ya, yb, yc = (jnp.concatenate(pair, axis=0) for pair in ((ya, ya_c), (yb, yb_c), (yc, yc_c)))
        n_tiles = rows.lat_tiles if last else rows.all_tiles
        x1, h2, ri, rw = _merge_call(rows, n_tiles, ya, yb, yc, gates, x_all, w["w_br_a"][l], w["w_br_b"][l],
                                     w["w_br_c"][l], w["w_out"][l], mod3, p["g_ffn"][l][None, :],
                                     w["w_router"][l], w["b_router"][l])
        dest, buf_t, blk_e, n_used = _route(ri[:, :TOP_K])
        h2_pad = jnp.concatenate([h2, jnp.zeros((1, d), BF16)], axis=0)
        y = _moe_call(h2_pad[buf_t], blk_e, n_used, w["w_gate_up"][l], w["b_gate_up"][l],
                      w["w_down"][l], w["b_down"][l])
        yg = y[dest].reshape(n_tiles * TM, TOP_K * d)
        x_new = _combine_call(rows, n_tiles, x1, yg, rw, mod3, p["g_final"][None, :], last)
        if last:
            out = x_new.reshape(bsz, n, d)
        else:
            x_all = x_new
    return out


def kernel(x, c, ctx, c_ctx, w_mod, b_mod, g_mix, w_in, lam_q1, lam_k1, lam_q2, lam_k2, g_subln, rpb, g_q_a, w_q_b, g_kv_a, w_kv_b, w_br_a, w_br_b, w_br_c, w_out, g_ffn, w_router, b_router, w_gate_up, b_gate_up, w_down, b_down, g_final):
    return _trunk(dict(x=x, c=c, ctx=ctx, c_ctx=c_ctx, w_mod=w_mod, b_mod=b_mod, g_mix=g_mix, w_in=w_in,
                       lam_q1=lam_q1, lam_k1=lam_k1, lam_q2=lam_q2, lam_k2=lam_k2, g_subln=g_subln, rpb=rpb,
                       g_q_a=g_q_a, w_q_b=w_q_b, g_kv_a=g_kv_a, w_kv_b=w_kv_b, w_br_a=w_br_a, w_br_b=w_br_b,
                       w_br_c=w_br_c, w_out=w_out, g_ffn=g_ffn, w_router=w_router, b_router=b_router,
                       w_gate_up=w_gate_up, b_gate_up=b_gate_up, w_down=w_down, b_down=b_down, g_final=g_final))
```

```python
import functools
import math

import jax
import jax.numpy as jnp
from jax import lax
from jax.experimental import pallas as pl
from jax.experimental.pallas import tpu as pltpu

GRID_W = 64
A_HEADS = 4
A_HEAD_DIM = 64
A_WIDTH = A_HEADS * 2 * A_HEAD_DIM
B_HEADS = 8
B_HEAD_DIM = 64
B_WIDTH = B_HEADS * B_HEAD_DIM
NA_ROWS = 8
NA_COLS = 16
C_HEADS = 8
C_NOPE = 64
C_ROPE = 32
C_VDIM = 64
C_Q_RANK = 384
C_KV_RANK = 256
C_WIDTH = C_HEADS * C_VDIM
N_BRANCH = 3
N_EXPERTS = 32
TOP_K = 4
SWIGLU_LIMIT = 7.0
SWIGLU_ALPHA = 1.702
N_MOD = 6
ROPE_BASE = 10000.0
EPS = 1e-6
NEG_INF = -1e30
LOG2E = 1.4426950408889634

LANES = 128
VMEM_LIMIT = 56 * 1024 * 1024
TM = 256
TQ = 512
TK = 2048
MOE_TB = 512
NB_ROWS_PER_STEP = 8

BF16 = jnp.bfloat16
F32 = jnp.float32


def _cparams(sem):
    return pltpu.CompilerParams(dimension_semantics=sem, vmem_limit_bytes=VMEM_LIMIT)


def _dot(a, b):
    return jnp.dot(a, b, preferred_element_type=F32)


def _dot_nt(a, b):
    return lax.dot_general(a, b, (((1,), (1,)), ((), ())), preferred_element_type=F32)


def _split_bf16(a):
    hi = a.astype(BF16)
    lo = (a - hi.astype(F32)).astype(BF16)
    return hi, lo


def _dot_split(a, b):
    a_hi, a_lo = _split_bf16(a)
    b_hi, b_lo = _split_bf16(b)
    return _dot(a_hi, b_hi) + _dot(a_hi, b_lo) + _dot(a_lo, b_hi)


def _rms(x):
    return x * lax.rsqrt(jnp.mean(x * x, axis=-1, keepdims=True) + EPS)


def _modulated(x_ref, g_ref, sh_ref, sc_ref):
    return (_rms(x_ref[...]) * g_ref[...]) * (1.0 + sc_ref[0]) + sh_ref[0]


def _rope(blk, cos, sin_signed, shift):
    lane = lax.broadcasted_iota(jnp.int32, blk.shape, 1)
    first = ((lane // shift) % 2) == 0
    partner = jnp.where(first, pltpu.roll(blk, LANES - shift, 1), pltpu.roll(blk, shift, 1))
    return blk * cos + partner * sin_signed


def _mod_kernel(c_ref, w_ref, b_ref, o_ref):
    cc = c_ref[...]
    a = cc * jax.nn.sigmoid(cc)
    o_ref[...] = _dot_split(a, w_ref[...]) + b_ref[...]


def _mod_call(c8, w_mod, b_mod):
    d = c8.shape[1]
    n_out = w_mod.shape[1]
    return pl.pallas_call(
        _mod_kernel,
        grid=(n_out // d,),
        in_specs=[pl.BlockSpec((8, d), lambda j: (0, 0)),
                  pl.BlockSpec((d, d), lambda j: (0, j)),
                  pl.BlockSpec((1, d), lambda j: (0, j))],
        out_specs=pl.BlockSpec((8, d), lambda j: (0, j)),
        out_shape=jax.ShapeDtypeStruct((8, n_out), F32),
        compiler_params=_cparams(("arbitrary",)),
        name="mod",
    )(c8, w_mod, b_mod.reshape(1, n_out))


class _Rows:
    def __init__(self, bsz, n, n_ctx):
        assert n % TM == 0 and (bsz * n_ctx) % TM == 0
        self.bsz, self.n, self.n_ctx = bsz, n, n_ctx
        self.t_lat = bsz * n
        self.t_all = bsz * n + bsz * n_ctx
        self.tiles_per_batch = n // TM
        self.lat_tiles = self.t_lat // TM
        self.all_tiles = self.t_all // TM

    def mod_spec(self, which, d):
        tpb, bsz = self.tiles_per_batch, self.bsz
        return pl.BlockSpec((1, 1, d), lambda i, *_: (jnp.minimum(i // tpb, bsz) * N_MOD + which, 0, 0))

    def rope_spec(self):
        tpb, lat = self.tiles_per_batch, self.lat_tiles
        return pl.BlockSpec((TM, LANES), lambda i, *_: (jnp.where(i < lat, i % tpb, tpb), 0))


def _inproj_qkv_kernel(*refs, rope, qscale):
    if rope:
        x_ref, g_ref, sh_ref, sc_ref, w_ref, cos_ref, sin_ref, q_ref, k_ref, v_ref = refs
    else:
        x_ref, g_ref, sh_ref, sc_ref, w_ref, q_ref, k_ref, v_ref = refs
    h = _modulated(x_ref, g_ref, sh_ref, sc_ref).astype(BF16)
    t = _dot(h, w_ref[...])
    outs = (q_ref, k_ref, v_ref)
    per = q_ref.shape[1] // LANES
    for j in range(3 * per):
        blk = t[:, j * LANES:(j + 1) * LANES]
        if rope and j < 2 * per:
            blk = _rope(blk, cos_ref[...], sin_ref[...], A_HEAD_DIM // 4)
        if j < per:
            blk = blk * qscale
        outs[j // per][:, (j % per) * LANES:(j % per + 1) * LANES] = blk.astype(BF16)


def _inproj_qkv_call(rows, x_all, g, mod3, w, tables, qscale):
    d = x_all.shape[1]
    width = w.shape[1] // 3
    rope = tables is not None
    in_specs = [pl.BlockSpec((TM, d), lambda i: (i, 0)),
                pl.BlockSpec((1, d), lambda i: (0, 0)),
                rows.mod_spec(0, d), rows.mod_spec(1, d),
                pl.BlockSpec((d, 3 * width), lambda i: (0, 0))]
    args = [x_all, g, mod3, mod3, w]
    if rope:
        in_specs += [rows.rope_spec(), rows.rope_spec()]
        args += list(tables)
    out = jax.ShapeDtypeStruct((rows.t_all, width), BF16)
    return pl.pallas_call(
        functools.partial(_inproj_qkv_kernel, rope=rope, qscale=qscale),
        grid=(rows.all_tiles,),
        in_specs=in_specs,
        out_specs=[pl.BlockSpec((TM, width), lambda i: (i, 0))] * 3,
        out_shape=[out] * 3,
        compiler_params=_cparams(("parallel",)),
        name="inproj_qkv_rope" if rope else "inproj_qkv",
    )(*args)


def _inproj_gates_kernel(x_ref, g_ref, sh_ref, sc_ref, w_ref, o_ref):
    h = _modulated(x_ref, g_ref, sh_ref, sc_ref).astype(BF16)
    d = x_ref.shape[1]
    for j in range(o_ref.shape[1] // d):
        o_ref[:, j * d:(j + 1) * d] = jax.nn.sigmoid(_dot(h, w_ref[:, j * d:(j + 1) * d])).astype(BF16)


def _inproj_gates_call(rows, x_all, g, mod3, w):
    d = x_all.shape[1]
    n_out = w.shape[1]
    return pl.pallas_call(
        _inproj_gates_kernel,
        grid=(rows.all_tiles,),
        in_specs=[pl.BlockSpec((TM, d), lambda i: (i, 0)),
                  pl.BlockSpec((1, d), lambda i: (0, 0)),
                  rows.mod_spec(0, d), rows.mod_spec(1, d),
                  pl.BlockSpec((d, n_out), lambda i: (0, 0))],
        out_specs=pl.BlockSpec((TM, n_out), lambda i: (i, 0)),
        out_shape=jax.ShapeDtypeStruct((rows.t_all, n_out), BF16),
        compiler_params=_cparams(("parallel",)),
        name="inproj_gates",
    )(x_all, g, mod3, mod3, w)


def _inproj_mla_kernel(x_ref, g_ref, sh_ref, sc_ref, w_ref, gq_ref, wq_ref, gkv_ref, wkv_ref,
                       cos_ref, sin_ref, q_ref, k_ref, v_ref, *, qscale):
    h = _modulated(x_ref, g_ref, sh_ref, sc_ref).astype(BF16)
    t = _dot(h, w_ref[...])
    cos, sin = cos_ref[...], sin_ref[...]
    shift = C_ROPE // 4
    cq = (_rms(t[:, :C_Q_RANK]) * gq_ref[...]).astype(BF16)
    q = _dot(cq, wq_ref[...])
    for j in range(C_HEADS):
        blk = _rope(q[:, j * LANES:(j + 1) * LANES], cos, sin, shift) * qscale
        q_ref[:, j * LANES:(j + 1) * LANES] = blk.astype(BF16)
    ckv = (_rms(t[:, C_Q_RANK:C_Q_RANK + C_KV_RANK]) * gkv_ref[...]).astype(BF16)
    kv = _dot(ckv, wkv_ref[...])
    pe = _rope(t[:, C_Q_RANK + C_KV_RANK:], cos, sin, shift)
    for j in range(C_HEADS):
        k_ref[:, j * LANES:(j + 1) * LANES] = (kv[:, j * LANES:(j + 1) * LANES] + pe).astype(BF16)
    v_ref[...] = kv[:, C_HEADS * LANES:].astype(BF16)


def _inproj_mla_call(rows, x_all, g, mod3, w, gq, wq, gkv, wkv, tables, qscale):
    d = x_all.shape[1]
    full = lambda a: pl.BlockSpec(a.shape, lambda i: (0,) * a.ndim)
    qk = jax.ShapeDtypeStruct((rows.t_all, C_HEADS * LANES), BF16)
    return pl.pallas_call(
        functools.partial(_inproj_mla_kernel, qscale=qscale),
        grid=(rows.all_tiles,),
        in_specs=[pl.BlockSpec((TM, d), lambda i: (i, 0)), full(g),
                  rows.mod_spec(0, d), rows.mod_spec(1, d),
                  full(w), full(gq), full(wq), full(gkv), full(wkv),
                  rows.rope_spec(), rows.rope_spec()],
        out_specs=[pl.BlockSpec((TM, C_HEADS * LANES), lambda i: (i, 0)),
                   pl.BlockSpec((TM, C_HEADS * LANES), lambda i: (i, 0)),
                   pl.BlockSpec((TM, C_WIDTH), lambda i: (i, 0))],
        out_shape=[qk, qk, jax.ShapeDtypeStruct((rows.t_all, C_WIDTH), BF16)],
        compiler_params=_cparams(("parallel",)),
        name="inproj_mla",
    )(x_all, g, mod3, mod3, w, gq, wq, gkv, wkv, *tables)


def _pair_attention_kernel(*refs, n_lat_chunks, tk, mode, lam_init):
    if n_lat_chunks:
        q_ref, kl_ref, vl_ref, kc_ref, vc_ref, lam_ref, gs_ref, o_ref = refs
    else:
        q_ref, kc_ref, vc_ref, lam_ref, gs_ref, o_ref = refs
    q = q_ref[...]
    tq, wq = q.shape
    lane = lax.broadcasted_iota(jnp.int32, q.shape, 1)
    lower = lane < wq // 2
    zero = jnp.zeros_like(q)
    qs = jnp.concatenate([jnp.where(lower, q, zero), jnp.where(lower, zero, q)], axis=0)

    def step(kc, vc, carry):
        m, l, acc = carry
        s = _dot_nt(qs, kc)
        m_new = jnp.maximum(m, jnp.max(s, axis=-1, keepdims=True))
        p = jnp.exp2(s - m_new)
        alpha = jnp.exp2(m - m_new)
        l = alpha * l + jnp.sum(p, axis=-1, keepdims=True)
        acc = alpha * acc + _dot(p.astype(BF16), vc)
        return m_new, l, acc

    carry = (jnp.full((2 * tq, 1), NEG_INF, F32), jnp.zeros((2 * tq, 1), F32),
             jnp.zeros((2 * tq, LANES), F32))
    for i in range(n_lat_chunks):
        carry = step(kl_ref[i * tk:(i + 1) * tk, :], vl_ref[i * tk:(i + 1) * tk, :], carry)
    m, l, acc = step(kc_ref[...], vc_ref[...], carry)
    o = acc / l
    o0, o1 = o[:tq], o[tq:]
    if mode == "diff":
        lv = lam_ref[...]
        lam = (jnp.exp(jnp.sum(lv[0:1] * lv[1:2], axis=-1, keepdims=True))
               - jnp.exp(jnp.sum(lv[2:3] * lv[3:4], axis=-1, keepdims=True)) + lam_init)
        out = (_rms(o0 - lam * o1) * gs_ref[...]) * (1.0 - lam_init)
    else:
        out = jnp.where(lax.broadcasted_iota(jnp.int32, o0.shape, 1) < LANES // 2, o0, o1)
    o_ref[...] = out.astype(BF16)


def _pair_attention_call(rows, q, k, v, lamv, gs, *, wq, mode, lam_init, ctx_queries):
    bsz, n, n_ctx = rows.bsz, rows.n, rows.n_ctx
    n_pairs = v.shape[1] // LANES
    ctx_blk0 = rows.t_lat // n_ctx
    kc_spec = pl.BlockSpec((n_ctx, wq), lambda b, h, i: (ctx_blk0 + b, h))
    vc_spec = pl.BlockSpec((n_ctx, LANES), lambda b, h, i: (ctx_blk0 + b, h))
    small = [pl.BlockSpec(lamv.shape, lambda b, h, i: (0, 0)), pl.BlockSpec(gs.shape, lambda b, h, i: (0, 0))]
    if ctx_queries:
        tq, n_q, n_chunks, tk = n_ctx, 1, 0, 0
        q_spec = pl.BlockSpec((tq, wq), lambda b, h, i: (ctx_blk0 + b, h))
        o_spec = pl.BlockSpec((tq, LANES), lambda b, h, i: (b, h))
        in_specs = [q_spec, kc_spec, vc_spec] + small
        args = (q, k, v, lamv, gs)
        out_rows = bsz * n_ctx
    else:
        tk = min(TK, n)
        tq = min(TQ, n)
        assert n % tq == 0 and n % tk == 0
        n_q, n_chunks = n // tq, n // tk
        q_spec = pl.BlockSpec((tq, wq), lambda b, h, i: (b * n_q + i, h))
        o_spec = pl.BlockSpec((tq, LANES), lambda b, h, i: (b * n_q + i, h))
        in_specs = [q_spec,
                    pl.BlockSpec((n, wq), lambda b, h, i: (b, h)),
                    pl.BlockSpec((n, LANES), lambda b, h, i: (b, h)),
                    kc_spec, vc_spec] + small
        args = (q, k, v, k, v, lamv, gs)
        out_rows = rows.t_lat
    return pl.pallas_call(
        functools.partial(_pair_attention_kernel, n_lat_chunks=n_chunks, tk=tk, mode=mode, lam_init=lam_init),
        grid=(bsz, n_pairs, n_q),
        in_specs=in_specs,
        out_specs=o_spec,
        out_shape=jax.ShapeDtypeStruct((out_rows, n_pairs * LANES), BF16),
        compiler_params=_cparams(("parallel", "parallel", "arbitrary")),
        name=f"attn_{mode}_{'ctx' if ctx_queries else 'lat'}_{wq}",
    )(*args)


def _nbr_classes(n_rows):
    half = NA_ROWS // 2
    reps = list(range(half)) + [half] + list(range(n_rows - half + 1, n_rows))
    return reps


def _nbr_bias_table(rpb, n_rows):
    c_idx = jnp.arange(GRID_W, dtype=jnp.int32)
    col_start = jnp.clip(c_idx - NA_COLS // 2, 0, GRID_W - NA_COLS)
    col_mask = (c_idx[None, :] >= col_start[:, None]) & (c_idx[None, :] < col_start[:, None] + NA_COLS)
    col_bias_idx = jnp.clip(c_idx[None, :] - c_idx[:, None] + NA_COLS - 1, 0, 2 * NA_COLS - 2)
    rpb_c = rpb[:, :, col_bias_idx]
    tabs = []
    for r in _nbr_classes(n_rows):
        rs = min(max(r - NA_ROWS // 2, 0), n_rows - NA_ROWS)
        di = rs + jnp.arange(NA_ROWS) - r + NA_ROWS - 1
        bias = jnp.transpose(rpb_c[:, di], (0, 2, 1, 3))
        bias = jnp.where(col_mask[None, :, None, :], bias * LOG2E, NEG_INF)
        tabs.append(bias.reshape(rpb.shape[0], GRID_W, NA_ROWS * GRID_W))
    return jnp.stack(tabs).astype(F32)


def _nbr_attention_kernel(q_ref, kl_ref, vl_ref, kc_ref, vc_ref, bias_ref, o_ref, *, n_rows):
    half = NA_ROWS // 2
    kc, vc = kc_ref[...], vc_ref[...]
    win = NA_ROWS * GRID_W

    def one_row(j, _):
        r = pl.program_id(2) * NB_ROWS_PER_STEP + j
        rs = jnp.clip(r - half, 0, n_rows - NA_ROWS)
        cls = jnp.minimum(r, half) + jnp.maximum(r - (n_rows - half), 0)
        q = q_ref[pl.ds(pl.multiple_of(j * GRID_W, GRID_W), GRID_W), :]
        lane = lax.broadcasted_iota(jnp.int32, q.shape, 1)
        lower = lane < LANES // 2
        zero = jnp.zeros_like(q)
        qs = jnp.concatenate([jnp.where(lower, q, zero), jnp.where(lower, zero, q)], axis=0)
        off = pl.multiple_of(rs * GRID_W, GRID_W)
        kw = kl_ref[pl.ds(off, win), :]
        vw = vl_ref[pl.ds(off, win), :]
        bias = bias_ref[cls].reshape(2 * GRID_W, win)
        s = jnp.where(bias > 0.5 * NEG_INF, _dot_nt(qs, kw) + bias, NEG_INF)
        s_ctx = _dot_nt(qs, kc)
        m = jnp.maximum(jnp.max(s, axis=-1, keepdims=True), jnp.max(s_ctx, axis=-1, keepdims=True))
        p = jnp.exp2(s - m)
        p_ctx = jnp.exp2(s_ctx - m)
        l = jnp.sum(p, axis=-1, keepdims=True) + jnp.sum(p_ctx, axis=-1, keepdims=True)
        o = (_dot(p.astype(BF16), vw) + _dot(p_ctx.astype(BF16), vc)) / l
        out = jnp.where(lax.broadcasted_iota(jnp.int32, (GRID_W, LANES), 1) < LANES // 2,
                        o[:GRID_W], o[GRID_W:])
        o_ref[pl.ds(pl.multiple_of(j * GRID_W, GRID_W), GRID_W), :] = out.astype(BF16)
        return 0

    lax.fori_loop(0, NB_ROWS_PER_STEP, one_row, 0, unroll=4)


def _nbr_attention_call(rows, q, k, v, bias):
    bsz, n, n_ctx = rows.bsz, rows.n, rows.n_ctx
    n_rows = n // GRID_W
    assert n_rows % NB_ROWS_PER_STEP == 0 and n_rows >= 2 * NA_ROWS
    steps = n_rows // NB_ROWS_PER_STEP
    n_pairs = B_HEADS // 2
    ctx_blk0 = rows.t_lat // n_ctx
    tq = NB_ROWS_PER_STEP * GRID_W
    n_cls = bias.shape[0]
    return pl.pallas_call(
        functools.partial(_nbr_attention_kernel, n_rows=n_rows),
        grid=(bsz, n_pairs, steps),
        in_specs=[pl.BlockSpec((tq, LANES), lambda b, h, i: (b * steps + i, h)),
                  pl.BlockSpec((n, LANES), lambda b, h, i: (b, h)),
                  pl.BlockSpec((n, LANES), lambda b, h, i: (b, h)),
                  pl.BlockSpec((n_ctx, LANES), lambda b, h, i: (ctx_blk0 + b, h)),
                  pl.BlockSpec((n_ctx, LANES), lambda b, h, i: (ctx_blk0 + b, h)),
                  pl.BlockSpec((n_cls, 2, GRID_W, NA_ROWS * GRID_W), lambda b, h, i: (0, h, 0, 0))],
        out_specs=pl.BlockSpec((tq, LANES), lambda b, h, i: (b * steps + i, h)),
        out_shape=jax.ShapeDtypeStruct((rows.t_lat, B_WIDTH), BF16),
        compiler_params=_cparams(("parallel", "parallel", "arbitrary")),
        name="attn_nbr",
    )(q, k, v, k, v, bias)


def _merge_kernel(ya_ref, yb_ref, yc_ref, g_ref, x_ref, wa_ref, wb_ref, wc_ref, wo_ref,
                  gt1_ref, gffn_ref, sh2_ref, sc2_ref, wr_ref, br_ref,
                  x1_ref, h2_ref, ri_ref, rw_ref, rk_ref, cnt_ref):
    d = x_ref.shape[1]

    @pl.when(pl.program_id(0) == 0)
    def _():
        cnt_ref[...] = jnp.zeros_like(cnt_ref)

    g = g_ref[...].astype(F32)
    m = (g[:, :d] * _dot(ya_ref[...], wa_ref[...])
         + g[:, d:2 * d] * _dot(yb_ref[...], wb_ref[...])
         + g[:, 2 * d:] * _dot(yc_ref[...], wc_ref[...]))
    y = _dot(m.astype(BF16), wo_ref[...])
    x1 = x_ref[...] + gt1_ref[0] * y
    x1_ref[...] = x1
    h2 = (_rms(x1) * gffn_ref[...]) * (1.0 + sc2_ref[0]) + sh2_ref[0]
    h2_ref[...] = h2.astype(BF16)
    logits = _dot_split(h2, wr_ref[...]) + br_ref[...]
    lane = lax.broadcasted_iota(jnp.int32, logits.shape, 1)
    lane_f = lane.astype(F32)
    work = jnp.where(lane < N_EXPERTS, logits, NEG_INF)
    idx_out = jnp.zeros(logits.shape, F32)
    val_out = jnp.zeros(logits.shape, F32)
    onehot = jnp.zeros(logits.shape, F32)
    top = None
    den = jnp.zeros((logits.shape[0], 1), F32)
    idxs = []
    for j in range(TOP_K):
        mx = jnp.max(work, axis=-1, keepdims=True)
        idx = jnp.min(jnp.where(work == mx, lane_f, float(LANES)), axis=-1, keepdims=True)
        if top is None:
            top = mx
        e = jnp.exp(mx - top)
        den = den + e
        idx_out = jnp.where(lane == j, idx, idx_out)
        val_out = jnp.where(lane == j, e, val_out)
        picked = lane_f == idx
        onehot = jnp.where(picked, 1.0, onehot)
        work = jnp.where(picked, NEG_INF, work)
        idxs.append(idx)
    ri_ref[...] = idx_out.astype(jnp.int32)
    rw_ref[...] = val_out / den
    tm = logits.shape[0]
    earlier = (lax.broadcasted_iota(jnp.int32, (tm, tm), 1) < lax.broadcasted_iota(jnp.int32, (tm, tm), 0))
    before = _dot(earlier.astype(BF16), onehot.astype(BF16)) + cnt_ref[0:1, :]
    rank_out = jnp.zeros(logits.shape, F32)
    for j in range(TOP_K):
        rank_j = jnp.sum(jnp.where(lane_f == idxs[j], before, 0.0), axis=-1, keepdims=True)
        rank_out = jnp.where(lane == j, rank_j, rank_out)
    rk_ref[...] = rank_out.astype(jnp.int32)
    cnt_ref[...] = cnt_ref[...] + jnp.sum(onehot, axis=0, keepdims=True)


def _merge_call(rows, n_tiles, ya, yb, yc, gates, x_all, wa, wb, wc, wo, mod3, gffn, wr, br):
    d = x_all.shape[1]
    t = n_tiles * TM
    full = lambda a: pl.BlockSpec(a.shape, lambda i: (0,) * a.ndim)
    row = lambda w: pl.BlockSpec((TM, w), lambda i: (i, 0))
    return pl.pallas_call(
        _merge_kernel,
        grid=(n_tiles,),
        in_specs=[row(ya.shape[1]), row(yb.shape[1]), row(yc.shape[1]), row(gates.shape[1]), row(d),
                  full(wa), full(wb), full(wc), full(wo),
                  rows.mod_spec(2, d), full(gffn), rows.mod_spec(3, d), rows.mod_spec(4, d),
                  full(wr), full(br)],
        out_specs=[row(d), row(d), row(LANES), row(LANES), row(LANES),
                   pl.BlockSpec((8, LANES), lambda i: (0, 0))],
        out_shape=[jax.ShapeDtypeStruct((t, d), F32), jax.ShapeDtypeStruct((t, d), BF16),
                   jax.ShapeDtypeStruct((t, LANES), jnp.int32), jax.ShapeDtypeStruct((t, LANES), F32),
                   jax.ShapeDtypeStruct((t, LANES), jnp.int32), jax.ShapeDtypeStruct((8, LANES), F32)],
        compiler_params=_cparams(("arbitrary",)),
        name="merge_router",
    )(ya, yb, yc, gates, x_all, wa, wb, wc, wo, mod3, gffn, mod3, mod3, wr, br)


def _moe_kernel(blk_e_ref, n_used_ref, x_ref, wgu_ref, bgu_ref, wd_ref, bd_ref, o_ref):
    i = pl.program_id(0)

    @pl.when(i < n_used_ref[0])
    def _():
        gu = _dot(x_ref[...], wgu_ref[0]) + bgu_ref[0]
        gate = jnp.minimum(gu, SWIGLU_LIMIT)
        glu = gate * jax.nn.sigmoid(SWIGLU_ALPHA * gate)
        up1 = jnp.clip(gu, -SWIGLU_LIMIT, SWIGLU_LIMIT) + 1.0
        up1 = jnp.concatenate([pltpu.roll(up1[:, j * LANES:(j + 1) * LANES], LANES - 1, 1)
                               for j in range(gu.shape[1] // LANES)], axis=1)
        act = (glu * up1).astype(BF16)
        o_ref[...] = _dot(act, wd_ref[0]) + bd_ref[0]

    @pl.when(i >= n_used_ref[0])
    def _():
        o_ref[...] = jnp.zeros_like(o_ref)


def _moe_weight_prep_kernel(wgu_ref, wd_ref, ogu_ref, od_ref, scr):
    @pl.when((pl.program_id(0) == 0) & (pl.program_id(1) == 0))
    def _():
        scr[...] = jnp.zeros_like(scr)

    ogu_ref[0] = wgu_ref[0].astype(BF16)
    for c in range(scr.shape[0]):
        cols = slice(c * LANES, (c + 1) * LANES)
        scr[c, pl.ds(0, wd_ref.shape[1], stride=2), :] = wd_ref[0, :, cols]
        od_ref[0, :, cols] = scr[c].astype(BF16)


def _moe_weight_prep(w_gate_up, w_down):
    nl, ne, d, f2 = w_gate_up.shape
    f = w_down.shape[2]
    halves = 2
    return pl.pallas_call(
        _moe_weight_prep_kernel,
        grid=(nl * ne, halves),
        in_specs=[pl.BlockSpec((1, d, f2 // halves), lambda e, h: (e, 0, h)),
                  pl.BlockSpec((1, f // halves, d), lambda e, h: (e, h, 0))],
        out_specs=[pl.BlockSpec((1, d, f2 // halves), lambda e, h: (e, 0, h)),
                   pl.BlockSpec((1, 2 * f // halves, d), lambda e, h: (e, h, 0))],
        out_shape=[jax.ShapeDtypeStruct((nl * ne, d, f2), BF16), jax.ShapeDtypeStruct((nl * ne, 2 * f, d), BF16)],
        scratch_shapes=[pltpu.VMEM((d // LANES, 2 * f // halves, LANES), F32)],
        compiler_params=_cparams(("arbitrary", "arbitrary")),
        name="moe_weight_prep",
    )(w_gate_up.reshape(nl * ne, d, f2), w_down.reshape(nl * ne, f, d))


def _moe_call(xg, blk_e, n_used, wgu, bgu, wd, bd):
    n_rows, d = xg.shape
    f2 = wgu.shape[2]
    n_blk = n_rows // MOE_TB
    last = lambda i, be, nu: jnp.minimum(i, nu[0] - 1)
    grid_spec = pltpu.PrefetchScalarGridSpec(
        num_scalar_prefetch=2,
        grid=(n_blk,),
        in_specs=[pl.BlockSpec((MOE_TB, d), lambda i, be, nu: (last(i, be, nu), 0)),
                  pl.BlockSpec((1, d, f2), lambda i, be, nu: (be[i], 0, 0)),
                  pl.BlockSpec((1, 1, f2), lambda i, be, nu: (be[i], 0, 0)),
                  pl.BlockSpec((1, f2, d), lambda i, be, nu: (be[i], 0, 0)),
                  pl.BlockSpec((1, 1, d), lambda i, be, nu: (be[i], 0, 0))],
        out_specs=pl.BlockSpec((MOE_TB, d), lambda i, be, nu: (i, 0)),
    )
    return pl.pallas_call(
        _moe_kernel,
        grid_spec=grid_spec,
        out_shape=jax.ShapeDtypeStruct((n_rows, d), F32),
        compiler_params=_cparams(("arbitrary",)),
        name="moe_experts",
    )(blk_e, n_used, xg, wgu, bgu, wd, bd)


def _combine_kernel(x_ref, y_ref, rw_ref, gt2_ref, gfin_ref, o_ref, *, final):
    d = x_ref.shape[1]
    y = y_ref[...]
    rw = rw_ref[...]
    s = y[:, :d] * rw[:, 0:1]
    for j in range(1, TOP_K):
        s = s + y[:, j * d:(j + 1) * d] * rw[:, j:j + 1]
    x2 = x_ref[...] + gt2_ref[0] * s
    if final:
        x2 = _rms(x2) * gfin_ref[...]
    o_ref[...] = x2


def _combine_call(rows, n_tiles, x1, yg, rw, mod3, gfin, final):
    d = x1.shape[1]
    return pl.pallas_call(
        functools.partial(_combine_kernel, final=final),
        grid=(n_tiles,),
        in_specs=[pl.BlockSpec((TM, d), lambda i: (i, 0)),
                  pl.BlockSpec((TM, TOP_K * d), lambda i: (i, 0)),
                  pl.BlockSpec((TM, LANES), lambda i: (i, 0)),
                  rows.mod_spec(5, d),
                  pl.BlockSpec((1, d), lambda i: (0, 0))],
        out_specs=pl.BlockSpec((TM, d), lambda i: (i, 0)),
        out_shape=jax.ShapeDtypeStruct((n_tiles * TM, d), F32),
        compiler_params=_cparams(("parallel",)),
        name="moe_combine_final" if final else "moe_combine",
    )(x1, yg, rw, mod3, gfin)


def _route(top_idx, rank, counts):
    t = top_idx.shape[0]
    m = t * TOP_K
    flat_e = top_idx.reshape(m)
    padded = ((counts + MOE_TB - 1) // MOE_TB) * MOE_TB
    pend = jnp.cumsum(padded)
    pstart = pend - padded
    dest = pstart[flat_e] + rank.reshape(m)
    n_blk = -(-m // MOE_TB) + N_EXPERTS
    n_rows = n_blk * MOE_TB
    flat_t = jnp.repeat(jnp.arange(t, dtype=jnp.int32), TOP_K)
    buf_t = jnp.zeros((n_rows,), jnp.int32).at[dest].set(flat_t)
    n_used = (pend[-1] // MOE_TB).astype(jnp.int32)
    blk_start = jnp.arange(n_blk, dtype=jnp.int32) * MOE_TB
    blk_e = jnp.clip(jnp.searchsorted(pend, blk_start, side="right"), 0, N_EXPERTS - 1).astype(jnp.int32)
    blk_e = jnp.where(jnp.arange(n_blk) < n_used, blk_e, blk_e[jnp.maximum(n_used - 1, 0)])
    return dest.reshape(t, TOP_K), buf_t, blk_e, n_used.reshape(1)


def _rope_table(n, rot_dim, lane_off, period):
    qd = rot_dim // 4
    pos = jnp.arange(n, dtype=jnp.int32)
    rws = (pos // GRID_W).astype(F32)
    cls = (pos % GRID_W).astype(F32)
    inv = ROPE_BASE ** (-jnp.arange(qd, dtype=F32) / qd)
    lane = jnp.arange(LANES, dtype=jnp.int32)
    w = (lane - lane_off) % period
    active = (lane >= lane_off) & (w < rot_dim) if period == LANES else jnp.ones((LANES,), bool)
    slot = w // qd
    freq = inv[w % qd]
    ang = jnp.where((slot < 2)[None, :], rws[:, None], cls[:, None]) * freq[None, :]
    sign = jnp.where(slot % 2 == 0, -1.0, 1.0)
    cos = jnp.where(active[None, :], jnp.cos(ang), 1.0)
    sin = jnp.where(active[None, :], jnp.sin(ang) * sign[None, :], 0.0)
    ident = (jnp.ones((TM, LANES), F32), jnp.zeros((TM, LANES), F32))
    return jnp.concatenate([cos, ident[0]], 0), jnp.concatenate([sin, ident[1]], 0)


def _prep_weights(p):
    w_in = p["w_in"]
    nl, d, _ = w_in.shape
    o = {}
    o["w_a"] = w_in[:, :, :3 * A_WIDTH].astype(BF16)
    o["w_b"] = w_in[:, :, 3 * A_WIDTH:3 * A_WIDTH + 3 * B_WIDTH].astype(BF16)
    c0 = 3 * A_WIDTH + 3 * B_WIDTH
    c1 = c0 + C_Q_RANK + C_KV_RANK
    zeros = lambda k: jnp.zeros((nl, d, k), F32)
    o["w_c"] = jnp.concatenate([w_in[:, :, c0:c1], zeros(C_NOPE), w_in[:, :, c1:c1 + C_ROPE],
                                zeros(LANES - C_NOPE - C_ROPE)], axis=-1).astype(BF16)
    o["w_g"] = w_in[:, :, c1 + C_ROPE:].astype(BF16)
    wq = p["w_q_b"].reshape(nl, C_Q_RANK, C_HEADS, C_NOPE + C_ROPE)
    wq = jnp.pad(wq, ((0, 0), (0, 0), (0, 0), (0, LANES - C_NOPE - C_ROPE)))
    o["w_q_b"] = wq.reshape(nl, C_Q_RANK, C_HEADS * LANES).astype(BF16)
    wkv = p["w_kv_b"].reshape(nl, C_KV_RANK, C_HEADS, C_NOPE + C_VDIM)
    wk = jnp.pad(wkv[..., :C_NOPE], ((0, 0), (0, 0), (0, 0), (0, LANES - C_NOPE)))
    o["w_kv_b"] = jnp.concatenate([wk.reshape(nl, C_KV_RANK, C_HEADS * LANES),
                                   wkv[..., C_NOPE:].reshape(nl, C_KV_RANK, C_WIDTH)], axis=-1).astype(BF16)
    for name in ("w_br_a", "w_br_b", "w_br_c", "w_out"):
        o[name] = p[name].astype(BF16)
    o["w_gate_up"], o["w_down"] = _moe_weight_prep(p["w_gate_up"], p["w_down"])
    o["b_gate_up"] = p["b_gate_up"].reshape(-1, 1, p["b_gate_up"].shape[-1])
    o["b_down"] = p["b_down"].reshape(-1, 1, p["b_down"].shape[-1])
    o["w_router"] = jnp.pad(p["w_router"], ((0, 0), (0, 0), (0, LANES - N_EXPERTS)))
    o["b_router"] = jnp.pad(p["b_router"], ((0, 0), (0, LANES - N_EXPERTS)))[:, None, :]
    return o


@jax.jit
def _trunk(p):
    x, ctx = p["x"], p["ctx"]
    bsz, n, d = x.shape
    n_ctx = ctx.shape[1]
    depth = p["w_mod"].shape[0]
    rows = _Rows(bsz, n, n_ctx)
    w = _prep_weights(p)
    rope_a = _rope_table(n, A_HEAD_DIM, 0, A_HEAD_DIM)
    rope_c = _rope_table(n, C_ROPE, C_NOPE, LANES)
    scale_a = A_HEAD_DIM ** -0.5 * LOG2E
    scale_b = B_HEAD_DIM ** -0.5 * LOG2E
    scale_c = (C_NOPE + C_ROPE) ** -0.5 * LOG2E
    c8 = jnp.zeros((8, d), F32).at[:bsz].set(p["c"]).at[bsz].set(p["c_ctx"])
    x_all = jnp.concatenate([x.reshape(bsz * n, d), ctx.reshape(bsz * n_ctx, d)], axis=0)
    dummy_lam = jnp.zeros((4, A_HEAD_DIM), F32)
    dummy_g = jnp.ones((1, LANES), F32)
    out = None
    for l in range(depth):
        last = l == depth - 1
        lam_init = 0.8 - 0.6 * math.exp(-0.3 * l)
        mod3 = _mod_call(c8, p["w_mod"][l], p["b_mod"][l]).reshape(8 * N_MOD, 1, d)
        g_mix = p["g_mix"][l][None, :]
        qa, ka, va = _inproj_qkv_call(rows, x_all, g_mix, mod3, w["w_a"][l], rope_a, scale_a)
        qb, kb, vb = _inproj_qkv_call(rows, x_all, g_mix, mod3, w["w_b"][l], None, scale_b)
        qc, kc, vc = _inproj_mla_call(rows, x_all, g_mix, mod3, w["w_c"][l], p["g_q_a"][l][None, :],
                                      w["w_q_b"][l], p["g_kv_a"][l][None, :], w["w_kv_b"][l], rope_c, scale_c)
        gates = _inproj_gates_call(rows, x_all, g_mix, mod3, w["w_g"][l])
        lamv = jnp.stack([p["lam_q1"][l], p["lam_k1"][l], p["lam_q2"][l], p["lam_k2"][l]])
        gs = p["g_subln"][l][None, :]
        attn = functools.partial(_pair_attention_call, rows)
        ya = attn(qa, ka, va, lamv, gs, wq=LANES, mode="diff", lam_init=lam_init, ctx_queries=False)
        yb = _nbr_attention_call(rows, qb, kb, vb, _nbr_bias_table(p["rpb"][l], n // GRID_W))
        yc = attn(qc, kc, vc, dummy_lam, dummy_g, wq=2 * LANES, mode="pair", lam_init=0.0, ctx_queries=False)
        if not last:
            ya_c = attn(qa, ka, va, lamv, gs, wq=LANES, mode="diff", lam_init=lam_init, ctx_queries=True)
            yb_c = attn(qb, kb, vb, dummy_lam, dummy_g, wq=LANES, mode="pair", lam_init=0.0, ctx_queries=True)
            yc_c = attn(qc, kc, vc, dummy_lam, dummy_g, wq=2 * LANES, mode="pair", lam_init=0.0, ctx_queries=True)
            ya, yb, yc = (jnp.concatenate(pair, axis=0) for pair in ((ya, ya_c), (yb, yb_c), (yc, yc_c)))
        n_tiles = rows.lat_tiles if last else rows.all_tiles
        x1, h2, ri, rw, rk, cnt = _merge_call(rows, n_tiles, ya, yb, yc, gates, x_all, w["w_br_a"][l], w["w_br_b"][l],
                                     w["w_br_c"][l], w["w_out"][l], mod3, p["g_ffn"][l][None, :],
                                     w["w_router"][l], w["b_router"][l])
        dest, buf_t, blk_e, n_used = _route(ri[:, :TOP_K], rk[:, :TOP_K], cnt[0, :N_EXPERTS].astype(jnp.int32))
        y = _moe_call(h2[buf_t], blk_e + l * N_EXPERTS, n_used, w["w_gate_up"], w["b_gate_up"],
                      w["w_down"], w["b_down"])
        yg = y[dest].reshape(n_tiles * TM, TOP_K * d)
        x_new = _combine_call(rows, n_tiles, x1, yg, rw, mod3, p["g_final"][None, :], last)
        if last:
            out = x_new.reshape(bsz, n, d)
        else:
            x_all = x_new
    return out


def kernel(x, c, ctx, c_ctx, w_mod, b_mod, g_mix, w_in, lam_q1, lam_k1, lam_q2, lam_k2, g_subln, rpb, g_q_a, w_q_b, g_kv_a, w_kv_b, w_br_a, w_br_b, w_br_c, w_out, g_ffn, w_router, b_router, w_gate_up, b_gate_up, w_down, b_down, g_final):
    return _trunk(dict(x=x, c=c, ctx=ctx, c_ctx=c_ctx, w_mod=w_mod, b_mod=b_mod, g_mix=g_mix, w_in=w_in,
                       lam_q1=lam_q1, lam_k1=lam_k1, lam_q2=lam_q2, lam_k2=lam_k2, g_subln=g_subln, rpb=rpb,
                       g_q_a=g_q_a, w_q_b=w_q_b, g_kv_a=g_kv_a, w_kv_b=w_kv_b, w_br_a=w_br_a, w_br_b=w_br_b,
                       w_br_c=w_br_c, w_out=w_out, g_ffn=g_ffn, w_router=w_router, b_router=b_router,
                       w_gate_up=w_gate_up, b_gate_up=b_gate_up, w_down=w_down, b_down=b_down, g_final=g_final))
```

```python
import functools
import math

import jax
import jax.numpy as jnp
from jax import lax
from jax.experimental import pallas as pl
from jax.experimental.pallas import tpu as pltpu
from jax.experimental.pallas import tpu_sc as plsc

GRID_W = 64
A_HEADS = 4
A_HEAD_DIM = 64
A_WIDTH = A_HEADS * 2 * A_HEAD_DIM
B_HEADS = 8
B_HEAD_DIM = 64
B_WIDTH = B_HEADS * B_HEAD_DIM
NA_ROWS = 8
NA_COLS = 16
C_HEADS = 8
C_NOPE = 64
C_ROPE = 32
C_VDIM = 64
C_Q_RANK = 384
C_KV_RANK = 256
C_WIDTH = C_HEADS * C_VDIM
N_BRANCH = 3
N_EXPERTS = 32
TOP_K = 4
SWIGLU_LIMIT = 7.0
SWIGLU_ALPHA = 1.702
N_MOD = 6
ROPE_BASE = 10000.0
EPS = 1e-6
NEG_INF = -1e30
LOG2E = 1.4426950408889634

LANES = 128
VMEM_LIMIT = 56 * 1024 * 1024
TM = 256
TQ = 512
TK = 2048
MOE_TB = 512
Y_SLABS = 4
SC_WINDOW = 128
NB_GROUP = 4
NB_GROUPS_PER_STEP = 4

BF16 = jnp.bfloat16
F32 = jnp.float32


def _cparams(sem):
    return pltpu.CompilerParams(dimension_semantics=sem, vmem_limit_bytes=VMEM_LIMIT)


def _dot(a, b):
    return jnp.dot(a, b, preferred_element_type=F32)


def _dot_nt(a, b):
    return lax.dot_general(a, b, (((1,), (1,)), ((), ())), preferred_element_type=F32)


def _split_bf16(a):
    hi = a.astype(BF16)
    lo = (a - hi.astype(F32)).astype(BF16)
    return hi, lo


def _dot_split(a, b):
    a_hi, a_lo = _split_bf16(a)
    b_hi, b_lo = _split_bf16(b)
    return _dot(a_hi, b_hi) + _dot(a_hi, b_lo) + _dot(a_lo, b_hi)


def _rms(x):
    return x * lax.rsqrt(jnp.mean(x * x, axis=-1, keepdims=True) + EPS)


def _modulated(x_ref, g_ref, sh_ref, sc_ref):
    return (_rms(x_ref[...]) * g_ref[...]) * (1.0 + sc_ref[0]) + sh_ref[0]


def _rope(blk, cos, sin_signed, shift):
    lane = lax.broadcasted_iota(jnp.int32, blk.shape, 1)
    first = ((lane // shift) % 2) == 0
    partner = jnp.where(first, pltpu.roll(blk, LANES - shift, 1), pltpu.roll(blk, shift, 1))
    return blk * cos + partner * sin_signed


def _mod_kernel(c_ref, w_ref, b_ref, o_ref):
    cc = c_ref[...]
    a = cc * jax.nn.sigmoid(cc)
    o_ref[...] = _dot_split(a, w_ref[...]) + b_ref[...]


def _mod_call(c8, w_mod, b_mod):
    d = c8.shape[1]
    n_out = w_mod.shape[1]
    return pl.pallas_call(
        _mod_kernel,
        grid=(n_out // d,),
        in_specs=[pl.BlockSpec((8, d), lambda j: (0, 0)),
                  pl.BlockSpec((d, d), lambda j: (0, j)),
                  pl.BlockSpec((1, d), lambda j: (0, j))],
        out_specs=pl.BlockSpec((8, d), lambda j: (0, j)),
        out_shape=jax.ShapeDtypeStruct((8, n_out), F32),
        compiler_params=_cparams(("arbitrary",)),
        name="mod",
    )(c8, w_mod, b_mod.reshape(1, n_out))


class _Rows:
    def __init__(self, bsz, n, n_ctx):
        assert n % TM == 0 and (bsz * n_ctx) % TM == 0
        self.bsz, self.n, self.n_ctx = bsz, n, n_ctx
        self.t_lat = bsz * n
        self.t_all = bsz * n + bsz * n_ctx
        self.tiles_per_batch = n // TM
        self.lat_tiles = self.t_lat // TM
        self.all_tiles = self.t_all // TM

    def mod_spec(self, which, d):
        tpb, bsz = self.tiles_per_batch, self.bsz
        return pl.BlockSpec((1, 1, d), lambda i, *_: (jnp.minimum(i // tpb, bsz) * N_MOD + which, 0, 0))

    def rope_spec(self):
        tpb, lat = self.tiles_per_batch, self.lat_tiles
        return pl.BlockSpec((TM, LANES), lambda i, *_: (jnp.where(i < lat, i % tpb, tpb), 0))


def _inproj_qkv_kernel(*refs, rope, qscale):
    if rope:
        x_ref, g_ref, sh_ref, sc_ref, w_ref, cos_ref, sin_ref, q_ref, k_ref, v_ref = refs
    else:
        x_ref, g_ref, sh_ref, sc_ref, w_ref, q_ref, k_ref, v_ref = refs
    h = _modulated(x_ref, g_ref, sh_ref, sc_ref).astype(BF16)
    t = _dot(h, w_ref[...])
    outs = (q_ref, k_ref, v_ref)
    per = q_ref.shape[1] // LANES
    for j in range(3 * per):
        blk = t[:, j * LANES:(j + 1) * LANES]
        if rope and j < 2 * per:
            blk = _rope(blk, cos_ref[...], sin_ref[...], A_HEAD_DIM // 4)
        if j < per:
            blk = blk * qscale
        outs[j // per][:, (j % per) * LANES:(j % per + 1) * LANES] = blk.astype(BF16)


def _inproj_qkv_call(rows, x_all, g, mod3, w, tables, qscale):
    d = x_all.shape[1]
    width = w.shape[1] // 3
    rope = tables is not None
    in_specs = [pl.BlockSpec((TM, d), lambda i: (i, 0)),
                pl.BlockSpec((1, d), lambda i: (0, 0)),
                rows.mod_spec(0, d), rows.mod_spec(1, d),
                pl.BlockSpec((d, 3 * width), lambda i: (0, 0))]
    args = [x_all, g, mod3, mod3, w]
    if rope:
        in_specs += [rows.rope_spec(), rows.rope_spec()]
        args += list(tables)
    out = jax.ShapeDtypeStruct((rows.t_all, width), BF16)
    return pl.pallas_call(
        functools.partial(_inproj_qkv_kernel, rope=rope, qscale=qscale),
        grid=(rows.all_tiles,),
        in_specs=in_specs,
        out_specs=[pl.BlockSpec((TM, width), lambda i: (i, 0))] * 3,
        out_shape=[out] * 3,
        compiler_params=_cparams(("parallel",)),
        name="inproj_qkv_rope" if rope else "inproj_qkv",
    )(*args)


def _inproj_gates_kernel(x_ref, g_ref, sh_ref, sc_ref, w_ref, o_ref):
    h = _modulated(x_ref, g_ref, sh_ref, sc_ref).astype(BF16)
    d = x_ref.shape[1]
    for j in range(o_ref.shape[1] // d):
        o_ref[:, j * d:(j + 1) * d] = jax.nn.sigmoid(_dot(h, w_ref[:, j * d:(j + 1) * d])).astype(BF16)


def _inproj_gates_call(rows, x_all, g, mod3, w):
    d = x_all.shape[1]
    n_out = w.shape[1]
    return pl.pallas_call(
        _inproj_gates_kernel,
        grid=(rows.all_tiles,),
        in_specs=[pl.BlockSpec((TM, d), lambda i: (i, 0)),
                  pl.BlockSpec((1, d), lambda i: (0, 0)),
                  rows.mod_spec(0, d), rows.mod_spec(1, d),
                  pl.BlockSpec((d, n_out), lambda i: (0, 0))],
        out_specs=pl.BlockSpec((TM, n_out), lambda i: (i, 0)),
        out_shape=jax.ShapeDtypeStruct((rows.t_all, n_out), BF16),
        compiler_params=_cparams(("parallel",)),
        name="inproj_gates",
    )(x_all, g, mod3, mod3, w)


def _inproj_mla_kernel(x_ref, g_ref, sh_ref, sc_ref, w_ref, gq_ref, wq_ref, gkv_ref, wkv_ref,
                       cos_ref, sin_ref, q_ref, k_ref, v_ref, *, qscale):
    h = _modulated(x_ref, g_ref, sh_ref, sc_ref).astype(BF16)
    t = _dot(h, w_ref[...])
    cos, sin = cos_ref[...], sin_ref[...]
    shift = C_ROPE // 4
    cq = (_rms(t[:, :C_Q_RANK]) * gq_ref[...]).astype(BF16)
    q = _dot(cq, wq_ref[...])
    for j in range(C_HEADS):
        blk = _rope(q[:, j * LANES:(j + 1) * LANES], cos, sin, shift) * qscale
        q_ref[:, j * LANES:(j + 1) * LANES] = blk.astype(BF16)
    ckv = (_rms(t[:, C_Q_RANK:C_Q_RANK + C_KV_RANK]) * gkv_ref[...]).astype(BF16)
    kv = _dot(ckv, wkv_ref[...])
    pe = _rope(t[:, C_Q_RANK + C_KV_RANK:], cos, sin, shift)
    for j in range(C_HEADS):
        k_ref[:, j * LANES:(j + 1) * LANES] = (kv[:, j * LANES:(j + 1) * LANES] + pe).astype(BF16)
    v_ref[...] = kv[:, C_HEADS * LANES:].astype(BF16)


def _inproj_mla_call(rows, x_all, g, mod3, w, gq, wq, gkv, wkv, tables, qscale):
    d = x_all.shape[1]
    full = lambda a: pl.BlockSpec(a.shape, lambda i: (0,) * a.ndim)
    qk = jax.ShapeDtypeStruct((rows.t_all, C_HEADS * LANES), BF16)
    return pl.pallas_call(
        functools.partial(_inproj_mla_kernel, qscale=qscale),
        grid=(rows.all_tiles,),
        in_specs=[pl.BlockSpec((TM, d), lambda i: (i, 0)), full(g),
                  rows.mod_spec(0, d), rows.mod_spec(1, d),
                  full(w), full(gq), full(wq), full(gkv), full(wkv),
                  rows.rope_spec(), rows.rope_spec()],
        out_specs=[pl.BlockSpec((TM, C_HEADS * LANES), lambda i: (i, 0)),
                   pl.BlockSpec((TM, C_HEADS * LANES), lambda i: (i, 0)),
                   pl.BlockSpec((TM, C_WIDTH), lambda i: (i, 0))],
        out_shape=[qk, qk, jax.ShapeDtypeStruct((rows.t_all, C_WIDTH), BF16)],
        compiler_params=_cparams(("parallel",)),
        name="inproj_mla",
    )(x_all, g, mod3, mod3, w, gq, wq, gkv, wkv, *tables)


def _pair_attention_kernel(*refs, n_lat_chunks, tk, mode, lam_init):
    if n_lat_chunks:
        q_ref, kl_ref, vl_ref, kc_ref, vc_ref, lam_ref, gs_ref, o_ref = refs
    else:
        q_ref, kc_ref, vc_ref, lam_ref, gs_ref, o_ref = refs
    q = q_ref[...]
    tq, wq = q.shape
    lane = lax.broadcasted_iota(jnp.int32, q.shape, 1)
    lower = lane < wq // 2
    zero = jnp.zeros_like(q)
    qs = jnp.concatenate([jnp.where(lower, q, zero), jnp.where(lower, zero, q)], axis=0)

    def step(kc, vc, carry):
        m, l, acc = carry
        s = _dot_nt(qs, kc)
        m_new = jnp.maximum(m, jnp.max(s, axis=-1, keepdims=True))
        p = jnp.exp2(s - m_new)
        alpha = jnp.exp2(m - m_new)
        l = alpha * l + jnp.sum(p, axis=-1, keepdims=True)
        acc = alpha * acc + _dot(p.astype(BF16), vc)
        return m_new, l, acc

    carry = (jnp.full((2 * tq, 1), NEG_INF, F32), jnp.zeros((2 * tq, 1), F32),
             jnp.zeros((2 * tq, LANES), F32))
    for i in range(n_lat_chunks):
        carry = step(kl_ref[i * tk:(i + 1) * tk, :], vl_ref[i * tk:(i + 1) * tk, :], carry)
    m, l, acc = step(kc_ref[...], vc_ref[...], carry)
    o = acc / l
    o0, o1 = o[:tq], o[tq:]
    if mode == "diff":
        lv = lam_ref[...]
        lam = (jnp.exp(jnp.sum(lv[0:1] * lv[1:2], axis=-1, keepdims=True))
               - jnp.exp(jnp.sum(lv[2:3] * lv[3:4], axis=-1, keepdims=True)) + lam_init)
        out = (_rms(o0 - lam * o1) * gs_ref[...]) * (1.0 - lam_init)
    else:
        out = jnp.where(lax.broadcasted_iota(jnp.int32, o0.shape, 1) < LANES // 2, o0, o1)
    o_ref[...] = out.astype(BF16)


def _pair_attention_call(rows, q, k, v, lamv, gs, *, wq, mode, lam_init, ctx_queries):
    bsz, n, n_ctx = rows.bsz, rows.n, rows.n_ctx
    n_pairs = v.shape[1] // LANES
    ctx_blk0 = rows.t_lat // n_ctx
    kc_spec = pl.BlockSpec((n_ctx, wq), lambda b, h, i: (ctx_blk0 + b, h))
    vc_spec = pl.BlockSpec((n_ctx, LANES), lambda b, h, i: (ctx_blk0 + b, h))
    small = [pl.BlockSpec(lamv.shape, lambda b, h, i: (0, 0)), pl.BlockSpec(gs.shape, lambda b, h, i: (0, 0))]
    if ctx_queries:
        tq, n_q, n_chunks, tk = n_ctx, 1, 0, 0
        q_spec = pl.BlockSpec((tq, wq), lambda b, h, i: (ctx_blk0 + b, h))
        o_spec = pl.BlockSpec((tq, LANES), lambda b, h, i: (b, h))
        in_specs = [q_spec, kc_spec, vc_spec] + small
        args = (q, k, v, lamv, gs)
        out_rows = bsz * n_ctx
    else:
        tk = min(TK, n)
        tq = min(TQ, n)
        assert n % tq == 0 and n % tk == 0
        n_q, n_chunks = n // tq, n // tk
        q_spec = pl.BlockSpec((tq, wq), lambda b, h, i: (b * n_q + i, h))
        o_spec = pl.BlockSpec((tq, LANES), lambda b, h, i: (b * n_q + i, h))
        in_specs = [q_spec,
                    pl.BlockSpec((n, wq), lambda b, h, i: (b, h)),
                    pl.BlockSpec((n, LANES), lambda b, h, i: (b, h)),
                    kc_spec, vc_spec] + small
        args = (q, k, v, k, v, lamv, gs)
        out_rows = rows.t_lat
    return pl.pallas_call(
        functools.partial(_pair_attention_kernel, n_lat_chunks=n_chunks, tk=tk, mode=mode, lam_init=lam_init),
        grid=(bsz, n_pairs, n_q),
        in_specs=in_specs,
        out_specs=o_spec,
        out_shape=jax.ShapeDtypeStruct((out_rows, n_pairs * LANES), BF16),
        compiler_params=_cparams(("parallel", "parallel", "arbitrary")),
        name=f"attn_{mode}_{'ctx' if ctx_queries else 'lat'}_{wq}",
    )(*args)


def _nbr_groups(n_rows):
    half = NA_ROWS // 2
    span = NA_ROWS + NB_GROUP - 1
    starts, classes, patterns = [], [], []
    for r0 in range(0, n_rows, NB_GROUP):
        rs = [min(max(r - half, 0), n_rows - NA_ROWS) for r in range(r0, r0 + NB_GROUP)]
        us = min(rs[0], n_rows - span)
        pat = (tuple(x - us for x in rs), tuple(r - us for r in range(r0, r0 + NB_GROUP)))
        if pat not in patterns:
            patterns.append(pat)
        starts.append(us)
        classes.append(patterns.index(pat))
    return starts, classes, patterns


def _nbr_bias_table(rpb, patterns):
    span = NA_ROWS + NB_GROUP - 1
    c_idx = jnp.arange(GRID_W, dtype=jnp.int32)
    col_start = jnp.clip(c_idx - NA_COLS // 2, 0, GRID_W - NA_COLS)
    col_mask = (c_idx[None, :] >= col_start[:, None]) & (c_idx[None, :] < col_start[:, None] + NA_COLS)
    col_bias_idx = jnp.clip(c_idx[None, :] - c_idx[:, None] + NA_COLS - 1, 0, 2 * NA_COLS - 2)
    rpb_c = rpb[:, :, col_bias_idx] * LOG2E
    u = jnp.arange(span)
    tabs = []
    for rs_rel, r_rel in patterns:
        per_row = []
        for a in range(NB_GROUP):
            valid = (u >= rs_rel[a]) & (u < rs_rel[a] + NA_ROWS)
            di = jnp.clip(u - r_rel[a] + NA_ROWS - 1, 0, 2 * NA_ROWS - 2)
            bias = jnp.transpose(rpb_c[:, di], (0, 2, 1, 3))
            ok = valid[None, None, :, None] & col_mask[None, :, None, :]
            per_row.append(jnp.where(ok, bias, NEG_INF).reshape(rpb.shape[0], GRID_W, span * GRID_W))
        tabs.append(jnp.concatenate(per_row, axis=1))
    return jnp.stack(tabs).astype(F32)


def _nbr_attention_kernel(us_ref, cls_ref, q_ref, kl_ref, vl_ref, kc_ref, vc_ref, bias_ref, o_ref):
    kc, vc = kc_ref[...], vc_ref[...]
    rows_q = NB_GROUP * GRID_W
    win = (NA_ROWS + NB_GROUP - 1) * GRID_W

    def one_group(j, _):
        g = pl.program_id(2) * NB_GROUPS_PER_STEP + j
        q = q_ref[pl.ds(pl.multiple_of(j * rows_q, rows_q), rows_q), :]
        lane = lax.broadcasted_iota(jnp.int32, q.shape, 1)
        lower = lane < LANES // 2
        zero = jnp.zeros_like(q)
        qs = jnp.concatenate([jnp.where(lower, q, zero), jnp.where(lower, zero, q)], axis=0)
        off = pl.multiple_of(us_ref[g] * GRID_W, GRID_W)
        kw = kl_ref[pl.ds(off, win), :]
        vw = vl_ref[pl.ds(off, win), :]
        bias = bias_ref[cls_ref[g]].reshape(2 * rows_q, win)
        s = jnp.where(bias > 0.5 * NEG_INF, _dot_nt(qs, kw) + bias, NEG_INF)
        s_ctx = _dot_nt(qs, kc)
        m = jnp.maximum(jnp.max(s, axis=-1, keepdims=True), jnp.max(s_ctx, axis=-1, keepdims=True))
        p = jnp.exp2(s - m)
        p_ctx = jnp.exp2(s_ctx - m)
        l = jnp.sum(p, axis=-1, keepdims=True) + jnp.sum(p_ctx, axis=-1, keepdims=True)
        o = (_dot(p.astype(BF16), vw) + _dot(p_ctx.astype(BF16), vc)) / l
        out = jnp.where(lax.broadcasted_iota(jnp.int32, (rows_q, LANES), 1) < LANES // 2,
                        o[:rows_q], o[rows_q:])
        o_ref[pl.ds(pl.multiple_of(j * rows_q, rows_q), rows_q), :] = out.astype(BF16)
        return 0

    lax.fori_loop(0, NB_GROUPS_PER_STEP, one_group, 0, unroll=True)


def _nbr_attention_call(rows, q, k, v, rpb):
    bsz, n, n_ctx = rows.bsz, rows.n, rows.n_ctx
    n_rows = n // GRID_W
    rows_per_step = NB_GROUP * NB_GROUPS_PER_STEP
    assert n_rows % rows_per_step == 0 and n_rows >= NA_ROWS + NB_GROUP - 1
    starts, classes, patterns = _nbr_groups(n_rows)
    bias = _nbr_bias_table(rpb, patterns)
    steps = n_rows // rows_per_step
    n_pairs = B_HEADS // 2
    ctx_blk0 = rows.t_lat // n_ctx
    tq = rows_per_step * GRID_W
    grid_spec = pltpu.PrefetchScalarGridSpec(
        num_scalar_prefetch=2,
        grid=(bsz, n_pairs, steps),
        in_specs=[pl.BlockSpec((tq, LANES), lambda b, h, i, us, cl: (b * steps + i, h)),
                  pl.BlockSpec((n, LANES), lambda b, h, i, us, cl: (b, h)),
                  pl.BlockSpec((n, LANES), lambda b, h, i, us, cl: (b, h)),
                  pl.BlockSpec((n_ctx, LANES), lambda b, h, i, us, cl: (ctx_blk0 + b, h)),
                  pl.BlockSpec((n_ctx, LANES), lambda b, h, i, us, cl: (ctx_blk0 + b, h)),
                  pl.BlockSpec((bias.shape[0], 2) + bias.shape[2:], lambda b, h, i, us, cl: (0, h, 0, 0))],
        out_specs=pl.BlockSpec((tq, LANES), lambda b, h, i, us, cl: (b * steps + i, h)),
    )
    return pl.pallas_call(
        _nbr_attention_kernel,
        grid_spec=grid_spec,
        out_shape=jax.ShapeDtypeStruct((rows.t_lat, B_WIDTH), BF16),
        compiler_params=_cparams(("parallel", "parallel", "arbitrary")),
        name="attn_nbr",
    )(jnp.asarray(starts, jnp.int32), jnp.asarray(classes, jnp.int32), q, k, v, k, v, bias)


def _merge_kernel(ya_ref, yb_ref, yc_ref, g_ref, x_ref, wa_ref, wb_ref, wc_ref, wo_ref,
                  gt1_ref, gffn_ref, sh2_ref, sc2_ref, wr_ref, br_ref,
                  x1_ref, h2_ref, ri_ref, rw_ref, rk_ref, cnt_ref):
    d = x_ref.shape[1]

    @pl.when(pl.program_id(0) == 0)
    def _():
        cnt_ref[...] = jnp.zeros_like(cnt_ref)

    g = g_ref[...].astype(F32)
    m = (g[:, :d] * _dot(ya_ref[...], wa_ref[...])
         + g[:, d:2 * d] * _dot(yb_ref[...], wb_ref[...])
         + g[:, 2 * d:] * _dot(yc_ref[...], wc_ref[...]))
    y = _dot(m.astype(BF16), wo_ref[...])
    x1 = x_ref[...] + gt1_ref[0] * y
    x1_ref[...] = x1
    h2 = (_rms(x1) * gffn_ref[...]) * (1.0 + sc2_ref[0]) + sh2_ref[0]
    hb = lax.bitcast_convert_type(h2.astype(BF16).astype(F32), jnp.uint32)
    words = hb[:, :d // 2] | (hb[:, d // 2:] >> 16)
    for c in range(h2_ref.shape[0]):
        h2_ref[c] = words[:, c * LANES:(c + 1) * LANES]
    logits = _dot_split(h2, wr_ref[...]) + br_ref[...]
    lane = lax.broadcasted_iota(jnp.int32, logits.shape, 1)
    lane_f = lane.astype(F32)
    work = jnp.where(lane < N_EXPERTS, logits, NEG_INF)
    idx_out = jnp.zeros(logits.shape, F32)
    val_out = jnp.zeros(logits.shape, F32)
    onehot = jnp.zeros(logits.shape, F32)
    top = None
    den = jnp.zeros((logits.shape[0], 1), F32)
    idxs = []
    for j in range(TOP_K):
        mx = jnp.max(work, axis=-1, keepdims=True)
        idx = jnp.min(jnp.where(work == mx, lane_f, float(LANES)), axis=-1, keepdims=True)
        if top is None:
            top = mx
        e = jnp.exp(mx - top)
        den = den + e
        idx_out = jnp.where(lane == j, idx, idx_out)
        val_out = jnp.where(lane == j, e, val_out)
        picked = lane_f == idx
        onehot = jnp.where(picked, 1.0, onehot)
        work = jnp.where(picked, NEG_INF, work)
        idxs.append(idx)
    ri_ref[...] = idx_out.astype(jnp.int32)
    rw_ref[...] = val_out / den
    tm = logits.shape[0]
    earlier = (lax.broadcasted_iota(jnp.int32, (tm, tm), 1) < lax.broadcasted_iota(jnp.int32, (tm, tm), 0))
    before = _dot(earlier.astype(BF16), onehot.astype(BF16)) + cnt_ref[0:1, :]
    rank_out = jnp.zeros(logits.shape, F32)
    for j in range(TOP_K):
        rank_j = jnp.sum(jnp.where(lane_f == idxs[j], before, 0.0), axis=-1, keepdims=True)
        rank_out = jnp.where(lane == j, rank_j, rank_out)
    rk_ref[...] = rank_out.astype(jnp.int32)
    cnt_ref[...] = cnt_ref[...] + jnp.sum(onehot, axis=0, keepdims=True)


def _merge_call(rows, n_tiles, ya, yb, yc, gates, x_all, wa, wb, wc, wo, mod3, gffn, wr, br):
    d = x_all.shape[1]
    t = n_tiles * TM
    full = lambda a: pl.BlockSpec(a.shape, lambda i: (0,) * a.ndim)
    row = lambda w: pl.BlockSpec((TM, w), lambda i: (i, 0))
    return pl.pallas_call(
        _merge_kernel,
        grid=(n_tiles,),
        in_specs=[row(ya.shape[1]), row(yb.shape[1]), row(yc.shape[1]), row(gates.shape[1]), row(d),
                  full(wa), full(wb), full(wc), full(wo),
                  rows.mod_spec(2, d), full(gffn), rows.mod_spec(3, d), rows.mod_spec(4, d),
                  full(wr), full(br)],
        out_specs=[row(d), pl.BlockSpec((d // 2 // LANES, TM, LANES), lambda i: (0, i, 0)),
                   row(LANES), row(LANES), row(LANES),
                   pl.BlockSpec((8, LANES), lambda i: (0, 0))],
        out_shape=[jax.ShapeDtypeStruct((t, d), F32), jax.ShapeDtypeStruct((d // 2 // LANES, t, LANES), jnp.uint32),
                   jax.ShapeDtypeStruct((t, LANES), jnp.int32), jax.ShapeDtypeStruct((t, LANES), F32),
                   jax.ShapeDtypeStruct((t, LANES), jnp.int32), jax.ShapeDtypeStruct((8, LANES), F32)],
        compiler_params=_cparams(("arbitrary",)),
        name="merge_router",
    )(ya, yb, yc, gates, x_all, wa, wb, wc, wo, mod3, gffn, mod3, mod3, wr, br)


def _moe_kernel(blk_e_ref, n_used_ref, valid_ref, x_ref, wgu_ref, bgu_ref, wd_ref, bd_ref, o_ref):
    i = pl.program_id(0)

    @pl.when(i < n_used_ref[0])
    def _():
        words = [x_ref[c] for c in range(x_ref.shape[0])]
        hi = [lax.bitcast_convert_type(w & jnp.uint32(0xFFFF0000), F32) for w in words]
        lo = [lax.bitcast_convert_type(w << 16, F32) for w in words]
        x = jnp.concatenate(hi + lo, axis=1)
        row = lax.broadcasted_iota(jnp.int32, x.shape, 0)
        x = jnp.where(row < valid_ref[i], x, 0.0).astype(BF16)
        gu = _dot(x, wgu_ref[0]) + bgu_ref[0]
        gate = jnp.minimum(gu, SWIGLU_LIMIT)
        glu = gate * jax.nn.sigmoid(SWIGLU_ALPHA * gate)
        up1 = jnp.clip(gu, -SWIGLU_LIMIT, SWIGLU_LIMIT) + 1.0
        up1 = jnp.concatenate([pltpu.roll(up1[:, j * LANES:(j + 1) * LANES], LANES - 1, 1)
                               for j in range(gu.shape[1] // LANES)], axis=1)
        act = (glu * up1).astype(BF16)
        y = _dot(act, wd_ref[0]) + bd_ref[0]
        width = o_ref.shape[2]
        for c in range(o_ref.shape[0]):
            o_ref[c] = y[:, c * width:(c + 1) * width]

    @pl.when(i >= n_used_ref[0])
    def _():
        o_ref[...] = jnp.zeros_like(o_ref)


def _moe_weight_prep_kernel(wgu_ref, wd_ref, ogu_ref, od_ref, scr):
    @pl.when((pl.program_id(0) == 0) & (pl.program_id(1) == 0))
    def _():
        scr[...] = jnp.zeros_like(scr)

    ogu_ref[0] = wgu_ref[0].astype(BF16)
    for c in range(scr.shape[0]):
        cols = slice(c * LANES, (c + 1) * LANES)
        scr[c, pl.ds(0, wd_ref.shape[1], stride=2), :] = wd_ref[0, :, cols]
        od_ref[0, :, cols] = scr[c].astype(BF16)


def _moe_weight_prep(w_gate_up, w_down):
    nl, ne, d, f2 = w_gate_up.shape
    f = w_down.shape[2]
    halves = 2
    return pl.pallas_call(
        _moe_weight_prep_kernel,
        grid=(nl * ne, halves),
        in_specs=[pl.BlockSpec((1, d, f2 // halves), lambda e, h: (e, 0, h)),
                  pl.BlockSpec((1, f // halves, d), lambda e, h: (e, h, 0))],
        out_specs=[pl.BlockSpec((1, d, f2 // halves), lambda e, h: (e, 0, h)),
                   pl.BlockSpec((1, 2 * f // halves, d), lambda e, h: (e, h, 0))],
        out_shape=[jax.ShapeDtypeStruct((nl * ne, d, f2), BF16), jax.ShapeDtypeStruct((nl * ne, 2 * f, d), BF16)],
        scratch_shapes=[pltpu.VMEM((d // LANES, 2 * f // halves, LANES), F32)],
        compiler_params=_cparams(("arbitrary", "arbitrary")),
        name="moe_weight_prep",
    )(w_gate_up.reshape(nl * ne, d, f2), w_down.reshape(nl * ne, f, d))


def _moe_call(xg, blk_e, n_used, blk_valid, wgu, bgu, wd, bd):
    n_pieces, n_rows, _ = xg.shape
    d, f2 = wgu.shape[1], wgu.shape[2]
    n_blk = n_rows // MOE_TB
    last = lambda i, be, nu, bv: jnp.minimum(i, nu[0] - 1)
    grid_spec = pltpu.PrefetchScalarGridSpec(
        num_scalar_prefetch=3,
        grid=(n_blk,),
        in_specs=[pl.BlockSpec((n_pieces, MOE_TB, LANES), lambda i, be, nu, bv: (0, last(i, be, nu, bv), 0)),
                  pl.BlockSpec((1, d, f2), lambda i, be, nu, bv: (be[i], 0, 0)),
                  pl.BlockSpec((1, 1, f2), lambda i, be, nu, bv: (be[i], 0, 0)),
                  pl.BlockSpec((1, f2, d), lambda i, be, nu, bv: (be[i], 0, 0)),
                  pl.BlockSpec((1, 1, d), lambda i, be, nu, bv: (be[i], 0, 0))],
        out_specs=pl.BlockSpec((Y_SLABS, MOE_TB, d // Y_SLABS), lambda i, be, nu, bv: (0, i, 0)),
    )
    return pl.pallas_call(
        _moe_kernel,
        grid_spec=grid_spec,
        out_shape=jax.ShapeDtypeStruct((Y_SLABS, n_rows, d // Y_SLABS), F32),
        compiler_params=_cparams(("arbitrary",)),
        name="moe_experts",
    )(blk_e, n_used, blk_valid, xg, wgu, bgu, wd, bd)


def _combine_kernel(x_ref, y_ref, rw_ref, gt2_ref, gfin_ref, o_ref, *, final):
    rw = rw_ref[...]
    s = None
    for j in range(TOP_K):
        y_j = jnp.concatenate([y_ref[c, j] for c in range(y_ref.shape[0])], axis=1) * rw[:, j:j + 1]
        s = y_j if s is None else s + y_j
    x2 = x_ref[...] + gt2_ref[0] * s
    if final:
        x2 = _rms(x2) * gfin_ref[...]
    o_ref[...] = x2


def _combine_call(rows, n_tiles, x1, yg, rw, mod3, gfin, final):
    d = x1.shape[1]
    return pl.pallas_call(
        functools.partial(_combine_kernel, final=final),
        grid=(n_tiles,),
        in_specs=[pl.BlockSpec((TM, d), lambda i: (i, 0)),
                  pl.BlockSpec((Y_SLABS, TOP_K, TM, d // Y_SLABS), lambda i: (0, 0, i, 0)),
                  pl.BlockSpec((TM, LANES), lambda i: (i, 0)),
                  rows.mod_spec(5, d),
                  pl.BlockSpec((1, d), lambda i: (0, 0))],
        out_specs=pl.BlockSpec((TM, d), lambda i: (i, 0)),
        out_shape=jax.ShapeDtypeStruct((n_tiles * TM, d), F32),
        compiler_params=_cparams(("parallel",)),
        name="moe_combine_final" if final else "moe_combine",
    )(x1, yg, rw, mod3, gfin)


def _sc_mesh():
    return plsc.VectorSubcoreMesh(core_axis_name="core", subcore_axis_name="subcore")


def _sc_scatter_rows(src, idx_list, n_out):
    r, c = src.shape
    assert r % SC_WINDOW == 0

    @pl.kernel(out_type=jax.ShapeDtypeStruct((n_out, c), src.dtype), mesh=_sc_mesh(), scratch_types=[])
    def scatter(x_hbm, *refs):
        idx_hbm, o_hbm = refs[:-1], refs[-1]

        def body(x_vmem, *idx_vmem):
            for iv in idx_vmem:
                pltpu.sync_copy(x_vmem, o_hbm.at[iv.at[0]])

        ispec = pl.BlockSpec((1, SC_WINDOW), lambda i: (0, i))
        pltpu.emit_pipeline(
            body, grid=(r // SC_WINDOW,),
            in_specs=[pl.BlockSpec((SC_WINDOW, c), lambda i: (i, 0))] + [ispec] * len(idx_hbm),
            out_specs=[], core_axis_name=("core", "subcore"),
            dimension_semantics=(pltpu.PARALLEL,))(x_hbm, *idx_hbm)

    return scatter(src, *idx_list)


def _sc_gather_rows(src, idx):
    r, c = idx.shape[1], src.shape[1]
    assert r % SC_WINDOW == 0

    @pl.kernel(out_type=jax.ShapeDtypeStruct((r, c), src.dtype), mesh=_sc_mesh(), scratch_types=[])
    def gather(x_hbm, i_hbm, o_hbm):
        def body(i_vmem, o_vmem):
            pltpu.sync_copy(x_hbm.at[i_vmem.at[0]], o_vmem)

        pltpu.emit_pipeline(
            body, grid=(r // SC_WINDOW,),
            in_specs=[pl.BlockSpec((1, SC_WINDOW), lambda i: (0, i))],
            out_specs=[pl.BlockSpec((SC_WINDOW, c), lambda i: (i, 0))],
            core_axis_name=("core", "subcore"),
            dimension_semantics=(pltpu.PARALLEL,))(i_hbm, o_hbm)

    return gather(src, idx)


def _route(top_idx, rank, counts):
    t = top_idx.shape[0]
    m = t * TOP_K
    padded = ((counts + MOE_TB - 1) // MOE_TB) * MOE_TB
    pend = jnp.cumsum(padded)
    pstart = pend - padded
    onehot = top_idx[:, :, None] == jnp.arange(N_EXPERTS, dtype=jnp.int32)[None, None, :]
    dest = jnp.sum(jnp.where(onehot, pstart[None, None, :], 0), axis=-1) + rank
    n_blk = -(-m // MOE_TB) + N_EXPERTS
    n_used = (pend[-1] // MOE_TB).astype(jnp.int32)
    blk = jnp.arange(n_blk, dtype=jnp.int32)
    blk_e = jnp.sum((blk[:, None] * MOE_TB >= pend[None, :]).astype(jnp.int32), axis=1)
    blk_e = jnp.minimum(blk_e, N_EXPERTS - 1)
    e_hot = blk_e[:, None] == jnp.arange(N_EXPERTS, dtype=jnp.int32)[None, :]
    first = jnp.sum(jnp.where(e_hot, pstart[None, :], 0), axis=1) // MOE_TB
    cnt_e = jnp.sum(jnp.where(e_hot, counts[None, :], 0), axis=1)
    blk_valid = jnp.clip(cnt_e - (blk - first) * MOE_TB, 0, MOE_TB).astype(jnp.int32)
    last_e = jnp.sum(jnp.where(blk == n_used - 1, blk_e, 0))
    blk_e = jnp.where(blk < n_used, blk_e, last_e).astype(jnp.int32)
    return dest, blk_e, n_used.reshape(1), blk_valid, n_blk * MOE_TB


def _rope_table(n, rot_dim, lane_off, period):
    qd = rot_dim // 4
    pos = jnp.arange(n, dtype=jnp.int32)
    rws = (pos // GRID_W).astype(F32)
    cls = (pos % GRID_W).astype(F32)
    inv = ROPE_BASE ** (-jnp.arange(qd, dtype=F32) / qd)
    lane = jnp.arange(LANES, dtype=jnp.int32)
    w = (lane - lane_off) % period
    active = (lane >= lane_off) & (w < rot_dim) if period == LANES else jnp.ones((LANES,), bool)
    slot = w // qd
    freq = inv[w % qd]
    ang = jnp.where((slot < 2)[None, :], rws[:, None], cls[:, None]) * freq[None, :]
    sign = jnp.where(slot % 2 == 0, -1.0, 1.0)
    cos = jnp.where(active[None, :], jnp.cos(ang), 1.0)
    sin = jnp.where(active[None, :], jnp.sin(ang) * sign[None, :], 0.0)
    ident = (jnp.ones((TM, LANES), F32), jnp.zeros((TM, LANES), F32))
    return jnp.concatenate([cos, ident[0]], 0), jnp.concatenate([sin, ident[1]], 0)


def _prep_weights(p):
    w_in = p["w_in"]
    nl, d, _ = w_in.shape
    o = {}
    o["w_a"] = w_in[:, :, :3 * A_WIDTH].astype(BF16)
    o["w_b"] = w_in[:, :, 3 * A_WIDTH:3 * A_WIDTH + 3 * B_WIDTH].astype(BF16)
    c0 = 3 * A_WIDTH + 3 * B_WIDTH
    c1 = c0 + C_Q_RANK + C_KV_RANK
    zeros = lambda k: jnp.zeros((nl, d, k), F32)
    o["w_c"] = jnp.concatenate([w_in[:, :, c0:c1], zeros(C_NOPE), w_in[:, :, c1:c1 + C_ROPE],
                                zeros(LANES - C_NOPE - C_ROPE)], axis=-1).astype(BF16)
    o["w_g"] = w_in[:, :, c1 + C_ROPE:].astype(BF16)
    wq = p["w_q_b"].reshape(nl, C_Q_RANK, C_HEADS, C_NOPE + C_ROPE)
    wq = jnp.pad(wq, ((0, 0), (0, 0), (0, 0), (0, LANES - C_NOPE - C_ROPE)))
    o["w_q_b"] = wq.reshape(nl, C_Q_RANK, C_HEADS * LANES).astype(BF16)
    wkv = p["w_kv_b"].reshape(nl, C_KV_RANK, C_HEADS, C_NOPE + C_VDIM)
    wk = jnp.pad(wkv[..., :C_NOPE], ((0, 0), (0, 0), (0, 0), (0, LANES - C_NOPE)))
    o["w_kv_b"] = jnp.concatenate([wk.reshape(nl, C_KV_RANK, C_HEADS * LANES),
                                   wkv[..., C_NOPE:].reshape(nl, C_KV_RANK, C_WIDTH)], axis=-1).astype(BF16)
    for name in ("w_br_a", "w_br_b", "w_br_c", "w_out"):
        o[name] = p[name].astype(BF16)
    o["w_gate_up"], o["w_down"] = _moe_weight_prep(p["w_gate_up"], p["w_down"])
    o["b_gate_up"] = p["b_gate_up"].reshape(-1, 1, p["b_gate_up"].shape[-1])
    o["b_down"] = p["b_down"].reshape(-1, 1, p["b_down"].shape[-1])
    o["w_router"] = jnp.pad(p["w_router"], ((0, 0), (0, 0), (0, LANES - N_EXPERTS)))
    o["b_router"] = jnp.pad(p["b_router"], ((0, 0), (0, LANES - N_EXPERTS)))[:, None, :]
    return o


@jax.jit
def _trunk(p):
    x, ctx = p["x"], p["ctx"]
    bsz, n, d = x.shape
    n_ctx = ctx.shape[1]
    depth = p["w_mod"].shape[0]
    rows = _Rows(bsz, n, n_ctx)
    w = _prep_weights(p)
    rope_a = _rope_table(n, A_HEAD_DIM, 0, A_HEAD_DIM)
    rope_c = _rope_table(n, C_ROPE, C_NOPE, LANES)
    scale_a = A_HEAD_DIM ** -0.5 * LOG2E
    scale_b = B_HEAD_DIM ** -0.5 * LOG2E
    scale_c = (C_NOPE + C_ROPE) ** -0.5 * LOG2E
    c8 = jnp.zeros((8, d), F32).at[:bsz].set(p["c"]).at[bsz].set(p["c_ctx"])
    x_all = jnp.concatenate([x.reshape(bsz * n, d), ctx.reshape(bsz * n_ctx, d)], axis=0)
    dummy_lam = jnp.zeros((4, A_HEAD_DIM), F32)
    dummy_g = jnp.ones((1, LANES), F32)
    out = None
    for l in range(depth):
        last = l == depth - 1
        lam_init = 0.8 - 0.6 * math.exp(-0.3 * l)
        mod3 = _mod_call(c8, p["w_mod"][l], p["b_mod"][l]).reshape(8 * N_MOD, 1, d)
        g_mix = p["g_mix"][l][None, :]
        qa, ka, va = _inproj_qkv_call(rows, x_all, g_mix, mod3, w["w_a"][l], rope_a, scale_a)
        qb, kb, vb = _inproj_qkv_call(rows, x_all, g_mix, mod3, w["w_b"][l], None, scale_b)
        qc, kc, vc = _inproj_mla_call(rows, x_all, g_mix, mod3, w["w_c"][l], p["g_q_a"][l][None, :],
                                      w["w_q_b"][l], p["g_kv_a"][l][None, :], w["w_kv_b"][l], rope_c, scale_c)
        gates = _inproj_gates_call(rows, x_all, g_mix, mod3, w["w_g"][l])
        lamv = jnp.stack([p["lam_q1"][l], p["lam_k1"][l], p["lam_q2"][l], p["lam_k2"][l]])
        gs = p["g_subln"][l][None, :]
        attn = functools.partial(_pair_attention_call, rows)
        ya = attn(qa, ka, va, lamv, gs, wq=LANES, mode="diff", lam_init=lam_init, ctx_queries=False)
        yb = _nbr_attention_call(rows, qb, kb, vb, p["rpb"][l])
        yc = attn(qc, kc, vc, dummy_lam, dummy_g, wq=2 * LANES, mode="pair", lam_init=0.0, ctx_queries=False)
        if not last:
            ya_c = attn(qa, ka, va, lamv, gs, wq=LANES, mode="diff", lam_init=lam_init, ctx_queries=True)
            yb_c = attn(qb, kb, vb, dummy_lam, dummy_g, wq=LANES, mode="pair", lam_init=0.0, ctx_queries=True)
            yc_c = attn(qc, kc, vc, dummy_lam, dummy_g, wq=2 * LANES, mode="pair", lam_init=0.0, ctx_queries=True)
            ya, yb, yc = (jnp.concatenate(pair, axis=0) for pair in ((ya, ya_c), (yb, yb_c), (yc, yc_c)))
        n_tiles = rows.lat_tiles if last else rows.all_tiles
        x1, h2, ri, rw, rk, cnt = _merge_call(rows, n_tiles, ya, yb, yc, gates, x_all, w["w_br_a"][l], w["w_br_b"][l],
                                     w["w_br_c"][l], w["w_out"][l], mod3, p["g_ffn"][l][None, :],
                                     w["w_router"][l], w["b_router"][l])
        dest, blk_e, n_used, blk_valid, n_rows = _route(ri[:, :TOP_K], rk[:, :TOP_K],
                                                        cnt[0, :N_EXPERTS].astype(jnp.int32))
        t_rows = n_tiles * TM
        n_pieces = h2.shape[0]
        piece_off = (jnp.arange(n_pieces, dtype=jnp.int32) * n_rows)[:, None]
        xg = _sc_scatter_rows(h2.reshape(n_pieces * t_rows, LANES),
                              [(piece_off + dest[None, :, j]).reshape(1, -1) for j in range(TOP_K)],
                              n_pieces * n_rows).reshape(n_pieces, n_rows, LANES)
        y = _moe_call(xg, blk_e + l * N_EXPERTS, n_used, blk_valid, w["w_gate_up"], w["b_gate_up"],
                      w["w_down"], w["b_down"])
        slab_off = (jnp.arange(Y_SLABS, dtype=jnp.int32) * n_rows)[:, None, None]
        yg = _sc_gather_rows(y.reshape(Y_SLABS * n_rows, d // Y_SLABS),
                             (slab_off + dest.T[None]).reshape(1, -1)).reshape(Y_SLABS, TOP_K, t_rows, d // Y_SLABS)
        x_new = _combine_call(rows, n_tiles, x1, yg, rw, mod3, p["g_final"][None, :], last)
        if last:
            out = x_new.reshape(bsz, n, d)
        else:
            x_all = x_new
    return out


def kernel(x, c, ctx, c_ctx, w_mod, b_mod, g_mix, w_in, lam_q1, lam_k1, lam_q2, lam_k2, g_subln, rpb, g_q_a, w_q_b, g_kv_a, w_kv_b, w_br_a, w_br_b, w_br_c, w_out, g_ffn, w_router, b_router, w_gate_up, b_gate_up, w_down, b_down, g_final):
    return _trunk(dict(x=x, c=c, ctx=ctx, c_ctx=c_ctx, w_mod=w_mod, b_mod=b_mod, g_mix=g_mix, w_in=w_in,
                       lam_q1=lam_q1, lam_k1=lam_k1, lam_q2=lam_q2, lam_k2=lam_k2, g_subln=g_subln, rpb=rpb,
                       g_q_a=g_q_a, w_q_b=w_q_b, g_kv_a=g_kv_a, w_kv_b=w_kv_b, w_br_a=w_br_a, w_br_b=w_br_b,
                       w_br_c=w_br_c, w_out=w_out, g_ffn=g_ffn, w_router=w_router, b_router=b_router,
                       w_gate_up=w_gate_up, b_gate_up=b_gate_up, w_down=w_down, b_down=b_down, g_final=g_final))
```

```python
import functools
import math

import jax
import jax.numpy as jnp
from jax import lax
from jax.experimental import pallas as pl
from jax.experimental.pallas import tpu as pltpu
from jax.experimental.pallas import tpu_sc as plsc

GRID_W = 64
A_HEADS = 4
A_HEAD_DIM = 64
A_WIDTH = A_HEADS * 2 * A_HEAD_DIM
B_HEADS = 8
B_HEAD_DIM = 64
B_WIDTH = B_HEADS * B_HEAD_DIM
NA_ROWS = 8
NA_COLS = 16
C_HEADS = 8
C_NOPE = 64
C_ROPE = 32
C_VDIM = 64
C_Q_RANK = 384
C_KV_RANK = 256
C_WIDTH = C_HEADS * C_VDIM
N_BRANCH = 3
N_EXPERTS = 32
TOP_K = 4
SWIGLU_LIMIT = 7.0
SWIGLU_ALPHA = 1.702
N_MOD = 6
ROPE_BASE = 10000.0
EPS = 1e-6
NEG_INF = -1e30
LOG2E = 1.4426950408889634

LANES = 128
VMEM_LIMIT = 56 * 1024 * 1024
TM = 512
TQ = 512
TK = 2048
MOE_TB = 512
Y_SLABS = 4
SC_WINDOW = 128
NB_GROUP = 4
NB_GROUPS_PER_STEP = 4

BF16 = jnp.bfloat16
F32 = jnp.float32


def _cparams(sem):
    return pltpu.CompilerParams(dimension_semantics=sem, vmem_limit_bytes=VMEM_LIMIT)


def _dot(a, b):
    return jnp.dot(a, b, preferred_element_type=F32)


def _dot_nt(a, b):
    return lax.dot_general(a, b, (((1,), (1,)), ((), ())), preferred_element_type=F32)


def _split_bf16(a):
    hi = a.astype(BF16)
    lo = (a - hi.astype(F32)).astype(BF16)
    return hi, lo


def _dot_split(a, b):
    a_hi, a_lo = _split_bf16(a)
    b_hi, b_lo = _split_bf16(b)
    return _dot(a_hi, b_hi) + _dot(a_hi, b_lo) + _dot(a_lo, b_hi)


def _rms(x):
    return x * lax.rsqrt(jnp.mean(x * x, axis=-1, keepdims=True) + EPS)


def _modulated(x_ref, g_ref, sh_ref, sc_ref):
    return (_rms(x_ref[...]) * g_ref[...]) * (1.0 + sc_ref[0]) + sh_ref[0]


def _rope(blk, cos, sin_signed, shift):
    lane = lax.broadcasted_iota(jnp.int32, blk.shape, 1)
    first = ((lane // shift) % 2) == 0
    partner = jnp.where(first, pltpu.roll(blk, LANES - shift, 1), pltpu.roll(blk, shift, 1))
    return blk * cos + partner * sin_signed


def _mod_kernel(c_ref, w_ref, b_ref, o_ref):
    cc = c_ref[...]
    a = cc * jax.nn.sigmoid(cc)
    o_ref[...] = _dot_split(a, w_ref[...]) + b_ref[...]


def _mod_call(c8, w_mod, b_mod):
    d = c8.shape[1]
    n_out = w_mod.shape[1]
    return pl.pallas_call(
        _mod_kernel,
        grid=(n_out // d,),
        in_specs=[pl.BlockSpec((8, d), lambda j: (0, 0)),
                  pl.BlockSpec((d, d), lambda j: (0, j)),
                  pl.BlockSpec((1, d), lambda j: (0, j))],
        out_specs=pl.BlockSpec((8, d), lambda j: (0, j)),
        out_shape=jax.ShapeDtypeStruct((8, n_out), F32),
        compiler_params=_cparams(("arbitrary",)),
        name="mod",
    )(c8, w_mod, b_mod.reshape(1, n_out))


class _Rows:
    def __init__(self, bsz, n, n_ctx):
        assert n % TM == 0 and (bsz * n_ctx) % TM == 0
        self.bsz, self.n, self.n_ctx = bsz, n, n_ctx
        self.t_lat = bsz * n
        self.t_all = bsz * n + bsz * n_ctx
        self.tiles_per_batch = n // TM
        self.lat_tiles = self.t_lat // TM
        self.all_tiles = self.t_all // TM

    def mod_spec(self, which, d):
        tpb, bsz = self.tiles_per_batch, self.bsz
        return pl.BlockSpec((1, 1, d), lambda i, *_: (jnp.minimum(i // tpb, bsz) * N_MOD + which, 0, 0))

    def rope_spec(self):
        tpb, lat = self.tiles_per_batch, self.lat_tiles
        return pl.BlockSpec((TM, LANES), lambda i, *_: (jnp.where(i < lat, i % tpb, tpb), 0))


def _inproj_qkv_kernel(*refs, rope, qscale):
    if rope:
        x_ref, g_ref, sh_ref, sc_ref, w_ref, cos_ref, sin_ref, q_ref, k_ref, v_ref = refs
    else:
        x_ref, g_ref, sh_ref, sc_ref, w_ref, q_ref, k_ref, v_ref = refs
    h = _modulated(x_ref, g_ref, sh_ref, sc_ref).astype(BF16)
    t = _dot(h, w_ref[...])
    outs = (q_ref, k_ref, v_ref)
    per = q_ref.shape[1] // LANES
    for j in range(3 * per):
        blk = t[:, j * LANES:(j + 1) * LANES]
        if rope and j < 2 * per:
            blk = _rope(blk, cos_ref[...], sin_ref[...], A_HEAD_DIM // 4)
        if j < per:
            blk = blk * qscale
        outs[j // per][:, (j % per) * LANES:(j % per + 1) * LANES] = blk.astype(BF16)


def _inproj_qkv_call(rows, x_all, g, mod3, w, tables, qscale):
    d = x_all.shape[1]
    width = w.shape[1] // 3
    rope = tables is not None
    in_specs = [pl.BlockSpec((TM, d), lambda i: (i, 0)),
                pl.BlockSpec((1, d), lambda i: (0, 0)),
                rows.mod_spec(0, d), rows.mod_spec(1, d),
                pl.BlockSpec((d, 3 * width), lambda i: (0, 0))]
    args = [x_all, g, mod3, mod3, w]
    if rope:
        in_specs += [rows.rope_spec(), rows.rope_spec()]
        args += list(tables)
    out = jax.ShapeDtypeStruct((rows.t_all, width), BF16)
    return pl.pallas_call(
        functools.partial(_inproj_qkv_kernel, rope=rope, qscale=qscale),
        grid=(rows.all_tiles,),
        in_specs=in_specs,
        out_specs=[pl.BlockSpec((TM, width), lambda i: (i, 0))] * 3,
        out_shape=[out] * 3,
        compiler_params=_cparams(("parallel",)),
        name="inproj_qkv_rope" if rope else "inproj_qkv",
    )(*args)


def _inproj_gates_kernel(x_ref, g_ref, sh_ref, sc_ref, w_ref, o_ref):
    h = _modulated(x_ref, g_ref, sh_ref, sc_ref).astype(BF16)
    d = x_ref.shape[1]
    for j in range(o_ref.shape[1] // d):
        o_ref[:, j * d:(j + 1) * d] = jax.nn.sigmoid(_dot(h, w_ref[:, j * d:(j + 1) * d])).astype(BF16)


def _inproj_gates_call(rows, x_all, g, mod3, w):
    d = x_all.shape[1]
    n_out = w.shape[1]
    return pl.pallas_call(
        _inproj_gates_kernel,
        grid=(rows.all_tiles,),
        in_specs=[pl.BlockSpec((TM, d), lambda i: (i, 0)),
                  pl.BlockSpec((1, d), lambda i: (0, 0)),
                  rows.mod_spec(0, d), rows.mod_spec(1, d),
                  pl.BlockSpec((d, n_out), lambda i: (0, 0))],
        out_specs=pl.BlockSpec((TM, n_out), lambda i: (i, 0)),
        out_shape=jax.ShapeDtypeStruct((rows.t_all, n_out), BF16),
        compiler_params=_cparams(("parallel",)),
        name="inproj_gates",
    )(x_all, g, mod3, mod3, w)


def _inproj_mla_kernel(x_ref, g_ref, sh_ref, sc_ref, w_ref, gq_ref, wq_ref, gkv_ref, wkv_ref,
                       cos_ref, sin_ref, q_ref, k_ref, v_ref, *, qscale):
    h = _modulated(x_ref, g_ref, sh_ref, sc_ref).astype(BF16)
    t = _dot(h, w_ref[...])
    cos, sin = cos_ref[...], sin_ref[...]
    shift = C_ROPE // 4
    cq = (_rms(t[:, :C_Q_RANK]) * gq_ref[...]).astype(BF16)
    q = _dot(cq, wq_ref[...])
    for j in range(C_HEADS):
        blk = _rope(q[:, j * LANES:(j + 1) * LANES], cos, sin, shift) * qscale
        q_ref[:, j * LANES:(j + 1) * LANES] = blk.astype(BF16)
    ckv = (_rms(t[:, C_Q_RANK:C_Q_RANK + C_KV_RANK]) * gkv_ref[...]).astype(BF16)
    kv = _dot(ckv, wkv_ref[...])
    pe = _rope(t[:, C_Q_RANK + C_KV_RANK:], cos, sin, shift)
    for j in range(C_HEADS):
        k_ref[:, j * LANES:(j + 1) * LANES] = (kv[:, j * LANES:(j + 1) * LANES] + pe).astype(BF16)
    v_ref[...] = kv[:, C_HEADS * LANES:].astype(BF16)


def _inproj_mla_call(rows, x_all, g, mod3, w, gq, wq, gkv, wkv, tables, qscale):
    d = x_all.shape[1]
    full = lambda a: pl.BlockSpec(a.shape, lambda i: (0,) * a.ndim)
    qk = jax.ShapeDtypeStruct((rows.t_all, C_HEADS * LANES), BF16)
    return pl.pallas_call(
        functools.partial(_inproj_mla_kernel, qscale=qscale),
        grid=(rows.all_tiles,),
        in_specs=[pl.BlockSpec((TM, d), lambda i: (i, 0)), full(g),
                  rows.mod_spec(0, d), rows.mod_spec(1, d),
                  full(w), full(gq), full(wq), full(gkv), full(wkv),
                  rows.rope_spec(), rows.rope_spec()],
        out_specs=[pl.BlockSpec((TM, C_HEADS * LANES), lambda i: (i, 0)),
                   pl.BlockSpec((TM, C_HEADS * LANES), lambda i: (i, 0)),
                   pl.BlockSpec((TM, C_WIDTH), lambda i: (i, 0))],
        out_shape=[qk, qk, jax.ShapeDtypeStruct((rows.t_all, C_WIDTH), BF16)],
        compiler_params=_cparams(("parallel",)),
        name="inproj_mla",
    )(x_all, g, mod3, mod3, w, gq, wq, gkv, wkv, *tables)


def _pair_attention_kernel(*refs, n_lat_chunks, tk, mode, lam_init):
    if n_lat_chunks:
        q_ref, kl_ref, vl_ref, kc_ref, vc_ref, lam_ref, gs_ref, o_ref = refs
    else:
        q_ref, kc_ref, vc_ref, lam_ref, gs_ref, o_ref = refs
    q = q_ref[...]
    tq, wq = q.shape
    lane = lax.broadcasted_iota(jnp.int32, q.shape, 1)
    lower = lane < wq // 2
    zero = jnp.zeros_like(q)
    qs = jnp.concatenate([jnp.where(lower, q, zero), jnp.where(lower, zero, q)], axis=0)

    def step(kc, vc, carry):
        m, l, acc = carry
        s = _dot_nt(qs, kc)
        m_new = jnp.maximum(m, jnp.max(s, axis=-1, keepdims=True))
        p = jnp.exp2(s - m_new)
        alpha = jnp.exp2(m - m_new)
        l = alpha * l + jnp.sum(p, axis=-1, keepdims=True)
        acc = alpha * acc + _dot(p.astype(BF16), vc)
        return m_new, l, acc

    carry = (jnp.full((2 * tq, 1), NEG_INF, F32), jnp.zeros((2 * tq, 1), F32),
             jnp.zeros((2 * tq, LANES), F32))
    for i in range(n_lat_chunks):
        carry = step(kl_ref[i * tk:(i + 1) * tk, :], vl_ref[i * tk:(i + 1) * tk, :], carry)
    m, l, acc = step(kc_ref[...], vc_ref[...], carry)
    o = acc / l
    o0, o1 = o[:tq], o[tq:]
    if mode == "diff":
        lv = lam_ref[...]
        lam = (jnp.exp(jnp.sum(lv[0:1] * lv[1:2], axis=-1, keepdims=True))
               - jnp.exp(jnp.sum(lv[2:3] * lv[3:4], axis=-1, keepdims=True)) + lam_init)
        out = (_rms(o0 - lam * o1) * gs_ref[...]) * (1.0 - lam_init)
    else:
        out = jnp.where(lax.broadcasted_iota(jnp.int32, o0.shape, 1) < LANES // 2, o0, o1)
    o_ref[...] = out.astype(BF16)


def _pair_attention_call(rows, q, k, v, lamv, gs, *, wq, mode, lam_init, ctx_queries):
    bsz, n, n_ctx = rows.bsz, rows.n, rows.n_ctx
    n_pairs = v.shape[1] // LANES
    ctx_blk0 = rows.t_lat // n_ctx
    kc_spec = pl.BlockSpec((n_ctx, wq), lambda b, h, i: (ctx_blk0 + b, h))
    vc_spec = pl.BlockSpec((n_ctx, LANES), lambda b, h, i: (ctx_blk0 + b, h))
    small = [pl.BlockSpec(lamv.shape, lambda b, h, i: (0, 0)), pl.BlockSpec(gs.shape, lambda b, h, i: (0, 0))]
    if ctx_queries:
        tq, n_q, n_chunks, tk = n_ctx, 1, 0, 0
        q_spec = pl.BlockSpec((tq, wq), lambda b, h, i: (ctx_blk0 + b, h))
        o_spec = pl.BlockSpec((tq, LANES), lambda b, h, i: (b, h))
        in_specs = [q_spec, kc_spec, vc_spec] + small
        args = (q, k, v, lamv, gs)
        out_rows = bsz * n_ctx
    else:
        tk = min(TK, n)
        tq = min(TQ, n)
        assert n % tq == 0 and n % tk == 0
        n_q, n_chunks = n // tq, n // tk
        q_spec = pl.BlockSpec((tq, wq), lambda b, h, i: (b * n_q + i, h))
        o_spec = pl.BlockSpec((tq, LANES), lambda b, h, i: (b * n_q + i, h))
        in_specs = [q_spec,
                    pl.BlockSpec((n, wq), lambda b, h, i: (b, h)),
                    pl.BlockSpec((n, LANES), lambda b, h, i: (b, h)),
                    kc_spec, vc_spec] + small
        args = (q, k, v, k, v, lamv, gs)
        out_rows = rows.t_lat
    return pl.pallas_call(
        functools.partial(_pair_attention_kernel, n_lat_chunks=n_chunks, tk=tk, mode=mode, lam_init=lam_init),
        grid=(bsz, n_pairs, n_q),
        in_specs=in_specs,
        out_specs=o_spec,
        out_shape=jax.ShapeDtypeStruct((out_rows, n_pairs * LANES), BF16),
        compiler_params=_cparams(("parallel", "parallel", "arbitrary")),
        name=f"attn_{mode}_{'ctx' if ctx_queries else 'lat'}_{wq}",
    )(*args)


def _nbr_groups(n_rows):
    half = NA_ROWS // 2
    span = NA_ROWS + NB_GROUP - 1
    starts, classes, patterns = [], [], []
    for r0 in range(0, n_rows, NB_GROUP):
        rs = [min(max(r - half, 0), n_rows - NA_ROWS) for r in range(r0, r0 + NB_GROUP)]
        us = min(rs[0], n_rows - span)
        pat = (tuple(x - us for x in rs), tuple(r - us for r in range(r0, r0 + NB_GROUP)))
        if pat not in patterns:
            patterns.append(pat)
        starts.append(us)
        classes.append(patterns.index(pat))
    return starts, classes, patterns


def _nbr_bias_table(rpb, patterns):
    span = NA_ROWS + NB_GROUP - 1
    c_idx = jnp.arange(GRID_W, dtype=jnp.int32)
    col_start = jnp.clip(c_idx - NA_COLS // 2, 0, GRID_W - NA_COLS)
    col_mask = (c_idx[None, :] >= col_start[:, None]) & (c_idx[None, :] < col_start[:, None] + NA_COLS)
    col_bias_idx = jnp.clip(c_idx[None, :] - c_idx[:, None] + NA_COLS - 1, 0, 2 * NA_COLS - 2)
    rpb_c = rpb[:, :, col_bias_idx] * LOG2E
    u = jnp.arange(span)
    tabs = []
    for rs_rel, r_rel in patterns:
        per_row = []
        for a in range(NB_GROUP):
            valid = (u >= rs_rel[a]) & (u < rs_rel[a] + NA_ROWS)
            di = jnp.clip(u - r_rel[a] + NA_ROWS - 1, 0, 2 * NA_ROWS - 2)
            bias = jnp.transpose(rpb_c[:, di], (0, 2, 1, 3))
            ok = valid[None, None, :, None] & col_mask[None, :, None, :]
            per_row.append(jnp.where(ok, bias, NEG_INF).reshape(rpb.shape[0], GRID_W, span * GRID_W))
        tabs.append(jnp.concatenate(per_row, axis=1))
    return jnp.stack(tabs).astype(F32)


def _nbr_attention_kernel(us_ref, cls_ref, q_ref, kl_ref, vl_ref, kc_ref, vc_ref, bias_ref, o_ref):
    kc, vc = kc_ref[...], vc_ref[...]
    rows_q = NB_GROUP * GRID_W
    win = (NA_ROWS + NB_GROUP - 1) * GRID_W

    def one_group(j, _):
        g = pl.program_id(2) * NB_GROUPS_PER_STEP + j
        q = q_ref[pl.ds(pl.multiple_of(j * rows_q, rows_q), rows_q), :]
        lane = lax.broadcasted_iota(jnp.int32, q.shape, 1)
        lower = lane < LANES // 2
        zero = jnp.zeros_like(q)
        qs = jnp.concatenate([jnp.where(lower, q, zero), jnp.where(lower, zero, q)], axis=0)
        off = pl.multiple_of(us_ref[g] * GRID_W, GRID_W)
        kw = kl_ref[pl.ds(off, win), :]
        vw = vl_ref[pl.ds(off, win), :]
        bias = bias_ref[cls_ref[g]].reshape(2 * rows_q, win)
        s = jnp.where(bias > 0.5 * NEG_INF, _dot_nt(qs, kw) + bias, NEG_INF)
        s_ctx = _dot_nt(qs, kc)
        m = jnp.maximum(jnp.max(s, axis=-1, keepdims=True), jnp.max(s_ctx, axis=-1, keepdims=True))
        p = jnp.exp2(s - m)
        p_ctx = jnp.exp2(s_ctx - m)
        l = jnp.sum(p, axis=-1, keepdims=True) + jnp.sum(p_ctx, axis=-1, keepdims=True)
        o = (_dot(p.astype(BF16), vw) + _dot(p_ctx.astype(BF16), vc)) / l
        out = jnp.where(lax.broadcasted_iota(jnp.int32, (rows_q, LANES), 1) < LANES // 2,
                        o[:rows_q], o[rows_q:])
        o_ref[pl.ds(pl.multiple_of(j * rows_q, rows_q), rows_q), :] = out.astype(BF16)
        return 0

    lax.fori_loop(0, NB_GROUPS_PER_STEP, one_group, 0, unroll=True)


def _nbr_attention_call(rows, q, k, v, rpb):
    bsz, n, n_ctx = rows.bsz, rows.n, rows.n_ctx
    n_rows = n // GRID_W
    rows_per_step = NB_GROUP * NB_GROUPS_PER_STEP
    assert n_rows % rows_per_step == 0 and n_rows >= NA_ROWS + NB_GROUP - 1
    starts, classes, patterns = _nbr_groups(n_rows)
    bias = _nbr_bias_table(rpb, patterns)
    steps = n_rows // rows_per_step
    n_pairs = B_HEADS // 2
    ctx_blk0 = rows.t_lat // n_ctx
    tq = rows_per_step * GRID_W
    grid_spec = pltpu.PrefetchScalarGridSpec(
        num_scalar_prefetch=2,
        grid=(bsz, n_pairs, steps),
        in_specs=[pl.BlockSpec((tq, LANES), lambda b, h, i, us, cl: (b * steps + i, h)),
                  pl.BlockSpec((n, LANES), lambda b, h, i, us, cl: (b, h)),
                  pl.BlockSpec((n, LANES), lambda b, h, i, us, cl: (b, h)),
                  pl.BlockSpec((n_ctx, LANES), lambda b, h, i, us, cl: (ctx_blk0 + b, h)),
                  pl.BlockSpec((n_ctx, LANES), lambda b, h, i, us, cl: (ctx_blk0 + b, h)),
                  pl.BlockSpec((bias.shape[0], 2) + bias.shape[2:], lambda b, h, i, us, cl: (0, h, 0, 0))],
        out_specs=pl.BlockSpec((tq, LANES), lambda b, h, i, us, cl: (b * steps + i, h)),
    )
    return pl.pallas_call(
        _nbr_attention_kernel,
        grid_spec=grid_spec,
        out_shape=jax.ShapeDtypeStruct((rows.t_lat, B_WIDTH), BF16),
        compiler_params=_cparams(("parallel", "parallel", "arbitrary")),
        name="attn_nbr",
    )(jnp.asarray(starts, jnp.int32), jnp.asarray(classes, jnp.int32), q, k, v, k, v, bias)


def _merge_kernel(ya_ref, yb_ref, yc_ref, g_ref, x_ref, wa_ref, wb_ref, wc_ref, wo_ref,
                  gt1_ref, gffn_ref, sh2_ref, sc2_ref, wr_ref, br_ref,
                  x1_ref, h2_ref, ri_ref, rw_ref, rk_ref, cnt_ref):
    d = x_ref.shape[1]

    @pl.when(pl.program_id(0) == 0)
    def _():
        cnt_ref[...] = jnp.zeros_like(cnt_ref)

    g = g_ref[...].astype(F32)
    m = (g[:, :d] * _dot(ya_ref[...], wa_ref[...])
         + g[:, d:2 * d] * _dot(yb_ref[...], wb_ref[...])
         + g[:, 2 * d:] * _dot(yc_ref[...], wc_ref[...]))
    y = _dot(m.astype(BF16), wo_ref[...])
    x1 = x_ref[...] + gt1_ref[0] * y
    x1_ref[...] = x1
    h2 = (_rms(x1) * gffn_ref[...]) * (1.0 + sc2_ref[0]) + sh2_ref[0]
    hb = lax.bitcast_convert_type(h2.astype(BF16).astype(F32), jnp.uint32)
    words = hb[:, :d // 2] | (hb[:, d // 2:] >> 16)
    for c in range(h2_ref.shape[0]):
        h2_ref[c] = words[:, c * LANES:(c + 1) * LANES]
    logits = _dot_split(h2, wr_ref[...]) + br_ref[...]
    lane = lax.broadcasted_iota(jnp.int32, logits.shape, 1)
    lane_f = lane.astype(F32)
    work = jnp.where(lane < N_EXPERTS, logits, NEG_INF)
    idx_out = jnp.zeros(logits.shape, F32)
    val_out = jnp.zeros(logits.shape, F32)
    onehot = jnp.zeros(logits.shape, F32)
    top = None
    den = jnp.zeros((logits.shape[0], 1), F32)
    idxs = []
    for j in range(TOP_K):
        mx = jnp.max(work, axis=-1, keepdims=True)
        idx = jnp.min(jnp.where(work == mx, lane_f, float(LANES)), axis=-1, keepdims=True)
        if top is None:
            top = mx
        e = jnp.exp(mx - top)
        den = den + e
        idx_out = jnp.where(lane == j, idx, idx_out)
        val_out = jnp.where(lane == j, e, val_out)
        picked = lane_f == idx
        onehot = jnp.where(picked, 1.0, onehot)
        work = jnp.where(picked, NEG_INF, work)
        idxs.append(idx)
    ri_ref[...] = idx_out.astype(jnp.int32)
    rw_ref[...] = val_out / den
    tm = logits.shape[0]
    earlier = (lax.broadcasted_iota(jnp.int32, (tm, tm), 1) < lax.broadcasted_iota(jnp.int32, (tm, tm), 0))
    before = _dot(earlier.astype(BF16), onehot.astype(BF16)) + cnt_ref[0:1, :]
    rank_out = jnp.zeros(logits.shape, F32)
    for j in range(TOP_K):
        rank_j = jnp.sum(jnp.where(lane_f == idxs[j], before, 0.0), axis=-1, keepdims=True)
        rank_out = jnp.where(lane == j, rank_j, rank_out)
    rk_ref[...] = rank_out.astype(jnp.int32)
    cnt_ref[...] = cnt_ref[...] + jnp.sum(onehot, axis=0, keepdims=True)


def _merge_call(rows, n_tiles, ya, yb, yc, gates, x_all, wa, wb, wc, wo, mod3, gffn, wr, br):
    d = x_all.shape[1]
    t = n_tiles * TM
    full = lambda a: pl.BlockSpec(a.shape, lambda i: (0,) * a.ndim)
    row = lambda w: pl.BlockSpec((TM, w), lambda i: (i, 0))
    return pl.pallas_call(
        _merge_kernel,
        grid=(n_tiles,),
        in_specs=[row(ya.shape[1]), row(yb.shape[1]), row(yc.shape[1]), row(gates.shape[1]), row(d),
                  full(wa), full(wb), full(wc), full(wo),
                  rows.mod_spec(2, d), full(gffn), rows.mod_spec(3, d), rows.mod_spec(4, d),
                  full(wr), full(br)],
        out_specs=[row(d), pl.BlockSpec((d // 2 // LANES, TM, LANES), lambda i: (0, i, 0)),
                   row(LANES), row(LANES), row(LANES),
                   pl.BlockSpec((8, LANES), lambda i: (0, 0))],
        out_shape=[jax.ShapeDtypeStruct((t, d), F32), jax.ShapeDtypeStruct((d // 2 // LANES, t, LANES), jnp.uint32),
                   jax.ShapeDtypeStruct((t, LANES), jnp.int32), jax.ShapeDtypeStruct((t, LANES), F32),
                   jax.ShapeDtypeStruct((t, LANES), jnp.int32), jax.ShapeDtypeStruct((8, LANES), F32)],
        compiler_params=_cparams(("arbitrary",)),
        name="merge_router",
    )(ya, yb, yc, gates, x_all, wa, wb, wc, wo, mod3, gffn, mod3, mod3, wr, br)


def _moe_kernel(blk_e_ref, n_used_ref, valid_ref, x_ref, wgu_ref, bgu_ref, wd_ref, bd_ref, o_ref):
    i = pl.program_id(0)

    @pl.when(i < n_used_ref[0])
    def _():
        words = [x_ref[c] for c in range(x_ref.shape[0])]
        hi = [lax.bitcast_convert_type(w & jnp.uint32(0xFFFF0000), F32) for w in words]
        lo = [lax.bitcast_convert_type(w << 16, F32) for w in words]
        x = jnp.concatenate(hi + lo, axis=1)
        row = lax.broadcasted_iota(jnp.int32, x.shape, 0)
        x = jnp.where(row < valid_ref[i], x, 0.0).astype(BF16)
        gu = _dot(x, wgu_ref[0]) + bgu_ref[0]
        acts = []
        for j in range(gu.shape[1] // (2 * LANES)):
            gate = jnp.minimum(gu[:, 2 * j * LANES:(2 * j + 1) * LANES], SWIGLU_LIMIT)
            up = jnp.clip(gu[:, (2 * j + 1) * LANES:(2 * j + 2) * LANES], -SWIGLU_LIMIT, SWIGLU_LIMIT)
            acts.append((up + 1.0) * (gate * jax.nn.sigmoid(SWIGLU_ALPHA * gate)))
        act = jnp.concatenate(acts, axis=1).astype(BF16)
        y = _dot(act, wd_ref[0]) + bd_ref[0]
        width = o_ref.shape[2]
        for c in range(o_ref.shape[0]):
            o_ref[c] = y[:, c * width:(c + 1) * width]

    @pl.when(i >= n_used_ref[0])
    def _():
        o_ref[...] = jnp.zeros_like(o_ref)


def _moe_weight_prep_kernel(wgu_ref, wd_ref, ogu_ref, od_ref):
    grp = 2 * LANES
    out_col = lax.broadcasted_iota(jnp.int32, (grp, grp), 1)
    src_col = jnp.where(out_col < LANES, 2 * out_col, 2 * (out_col - LANES) + 1)
    regroup = (lax.broadcasted_iota(jnp.int32, (grp, grp), 0) == src_col).astype(BF16)
    for j in range(wgu_ref.shape[2] // grp):
        cols = slice(j * grp, (j + 1) * grp)
        ogu_ref[0, :, cols] = _dot(wgu_ref[0, :, cols].astype(BF16), regroup).astype(BF16)
    od_ref[0] = wd_ref[0].astype(BF16)


def _regroup_gate_up(v):
    lead = v.shape[:-1]
    return jnp.swapaxes(v.reshape(lead + (-1, LANES, 2)), -1, -2).reshape(lead + (-1,))


def _moe_weight_prep(w_gate_up, w_down):
    nl, ne, d, f2 = w_gate_up.shape
    f = w_down.shape[2]
    halves = 2
    return pl.pallas_call(
        _moe_weight_prep_kernel,
        grid=(nl * ne, halves),
        in_specs=[pl.BlockSpec((1, d, f2 // halves), lambda e, h: (e, 0, h)),
                  pl.BlockSpec((1, f // halves, d), lambda e, h: (e, h, 0))],
        out_specs=[pl.BlockSpec((1, d, f2 // halves), lambda e, h: (e, 0, h)),
                   pl.BlockSpec((1, f // halves, d), lambda e, h: (e, h, 0))],
        out_shape=[jax.ShapeDtypeStruct((nl * ne, d, f2), BF16), jax.ShapeDtypeStruct((nl * ne, f, d), BF16)],
        compiler_params=_cparams(("parallel", "parallel")),
        name="moe_weight_prep",
    )(w_gate_up.reshape(nl * ne, d, f2), w_down.reshape(nl * ne, f, d))


def _moe_call(xg, blk_e, n_used, blk_valid, wgu, bgu, wd, bd):
    n_pieces, n_rows, _ = xg.shape
    d, f2, f = wgu.shape[1], wgu.shape[2], wd.shape[1]
    n_blk = n_rows // MOE_TB
    last = lambda i, be, nu, bv: jnp.minimum(i, nu[0] - 1)
    grid_spec = pltpu.PrefetchScalarGridSpec(
        num_scalar_prefetch=3,
        grid=(n_blk,),
        in_specs=[pl.BlockSpec((n_pieces, MOE_TB, LANES), lambda i, be, nu, bv: (0, last(i, be, nu, bv), 0)),
                  pl.BlockSpec((1, d, f2), lambda i, be, nu, bv: (be[i], 0, 0)),
                  pl.BlockSpec((1, 1, f2), lambda i, be, nu, bv: (be[i], 0, 0)),
                  pl.BlockSpec((1, f, d), lambda i, be, nu, bv: (be[i], 0, 0)),
                  pl.BlockSpec((1, 1, d), lambda i, be, nu, bv: (be[i], 0, 0))],
        out_specs=pl.BlockSpec((Y_SLABS, MOE_TB, d // Y_SLABS), lambda i, be, nu, bv: (0, i, 0)),
    )
    return pl.pallas_call(
        _moe_kernel,
        grid_spec=grid_spec,
        out_shape=jax.ShapeDtypeStruct((Y_SLABS, n_rows, d // Y_SLABS), F32),
        compiler_params=_cparams(("arbitrary",)),
        name="moe_experts",
    )(blk_e, n_used, blk_valid, xg, wgu, bgu, wd, bd)


def _combine_kernel(x_ref, y_ref, rw_ref, gt2_ref, gfin_ref, o_ref, *, final):
    rw = rw_ref[...]
    s = None
    for j in range(TOP_K):
        y_j = jnp.concatenate([y_ref[c, j] for c in range(y_ref.shape[0])], axis=1) * rw[:, j:j + 1]
        s = y_j if s is None else s + y_j
    x2 = x_ref[...] + gt2_ref[0] * s
    if final:
        x2 = _rms(x2) * gfin_ref[...]
    o_ref[...] = x2


def _combine_call(rows, n_tiles, x1, yg, rw, mod3, gfin, final):
    d = x1.shape[1]
    return pl.pallas_call(
        functools.partial(_combine_kernel, final=final),
        grid=(n_tiles,),
        in_specs=[pl.BlockSpec((TM, d), lambda i: (i, 0)),
                  pl.BlockSpec((Y_SLABS, TOP_K, TM, d // Y_SLABS), lambda i: (0, 0, i, 0)),
                  pl.BlockSpec((TM, LANES), lambda i: (i, 0)),
                  rows.mod_spec(5, d),
                  pl.BlockSpec((1, d), lambda i: (0, 0))],
        out_specs=pl.BlockSpec((TM, d), lambda i: (i, 0)),
        out_shape=jax.ShapeDtypeStruct((n_tiles * TM, d), F32),
        compiler_params=_cparams(("parallel",)),
        name="moe_combine_final" if final else "moe_combine",
    )(x1, yg, rw, mod3, gfin)


def _sc_mesh():
    return plsc.VectorSubcoreMesh(core_axis_name="core", subcore_axis_name="subcore")


def _sc_scatter_rows(src, idx_list, n_out):
    r, c = src.shape
    assert r % SC_WINDOW == 0

    @pl.kernel(out_type=jax.ShapeDtypeStruct((n_out, c), src.dtype), mesh=_sc_mesh(), scratch_types=[])
    def scatter(x_hbm, *refs):
        idx_hbm, o_hbm = refs[:-1], refs[-1]

        def body(x_vmem, *idx_vmem):
            for iv in idx_vmem:
                pltpu.sync_copy(x_vmem, o_hbm.at[iv.at[0]])

        ispec = pl.BlockSpec((1, SC_WINDOW), lambda i: (0, i))
        pltpu.emit_pipeline(
            body, grid=(r // SC_WINDOW,),
            in_specs=[pl.BlockSpec((SC_WINDOW, c), lambda i: (i, 0))] + [ispec] * len(idx_hbm),
            out_specs=[], core_axis_name=("core", "subcore"),
            dimension_semantics=(pltpu.PARALLEL,))(x_hbm, *idx_hbm)

    return scatter(src, *idx_list)


def _sc_gather_rows(src, idx):
    r, c = idx.shape[1], src.shape[1]
    assert r % SC_WINDOW == 0

    @pl.kernel(out_type=jax.ShapeDtypeStruct((r, c), src.dtype), mesh=_sc_mesh(), scratch_types=[])
    def gather(x_hbm, i_hbm, o_hbm):
        def body(i_vmem, o_vmem):
            pltpu.sync_copy(x_hbm.at[i_vmem.at[0]], o_vmem)

        pltpu.emit_pipeline(
            body, grid=(r // SC_WINDOW,),
            in_specs=[pl.BlockSpec((1, SC_WINDOW), lambda i: (0, i))],
            out_specs=[pl.BlockSpec((SC_WINDOW, c), lambda i: (i, 0))],
            core_axis_name=("core", "subcore"),
            dimension_semantics=(pltpu.PARALLEL,))(i_hbm, o_hbm)

    return gather(src, idx)


def _route(top_idx, rank, counts):
    t = top_idx.shape[0]
    m = t * TOP_K
    padded = ((counts + MOE_TB - 1) // MOE_TB) * MOE_TB
    pend = jnp.cumsum(padded)
    pstart = pend - padded
    onehot = top_idx[:, :, None] == jnp.arange(N_EXPERTS, dtype=jnp.int32)[None, None, :]
    dest = jnp.sum(jnp.where(onehot, pstart[None, None, :], 0), axis=-1) + rank
    n_blk = -(-m // MOE_TB) + N_EXPERTS
    n_used = (pend[-1] // MOE_TB).astype(jnp.int32)
    blk = jnp.arange(n_blk, dtype=jnp.int32)
    blk_e = jnp.sum((blk[:, None] * MOE_TB >= pend[None, :]).astype(jnp.int32), axis=1)
    blk_e = jnp.minimum(blk_e, N_EXPERTS - 1)
    e_hot = blk_e[:, None] == jnp.arange(N_EXPERTS, dtype=jnp.int32)[None, :]
    first = jnp.sum(jnp.where(e_hot, pstart[None, :], 0), axis=1) // MOE_TB
    cnt_e = jnp.sum(jnp.where(e_hot, counts[None, :], 0), axis=1)
    blk_valid = jnp.clip(cnt_e - (blk - first) * MOE_TB, 0, MOE_TB).astype(jnp.int32)
    last_e = jnp.sum(jnp.where(blk == n_used - 1, blk_e, 0))
    blk_e = jnp.where(blk < n_used, blk_e, last_e).astype(jnp.int32)
    return dest, blk_e, n_used.reshape(1), blk_valid, n_blk * MOE_TB


def _rope_table(n, rot_dim, lane_off, period):
    qd = rot_dim // 4
    pos = jnp.arange(n, dtype=jnp.int32)
    rws = (pos // GRID_W).astype(F32)
    cls = (pos % GRID_W).astype(F32)
    inv = ROPE_BASE ** (-jnp.arange(qd, dtype=F32) / qd)
    lane = jnp.arange(LANES, dtype=jnp.int32)
    w = (lane - lane_off) % period
    active = (lane >= lane_off) & (w < rot_dim) if period == LANES else jnp.ones((LANES,), bool)
    slot = w // qd
    freq = inv[w % qd]
    ang = jnp.where((slot < 2)[None, :], rws[:, None], cls[:, None]) * freq[None, :]
    sign = jnp.where(slot % 2 == 0, -1.0, 1.0)
    cos = jnp.where(active[None, :], jnp.cos(ang), 1.0)
    sin = jnp.where(active[None, :], jnp.sin(ang) * sign[None, :], 0.0)
    ident = (jnp.ones((TM, LANES), F32), jnp.zeros((TM, LANES), F32))
    return jnp.concatenate([cos, ident[0]], 0), jnp.concatenate([sin, ident[1]], 0)


def _prep_weights(p):
    w_in = p["w_in"]
    nl, d, _ = w_in.shape
    o = {}
    o["w_a"] = w_in[:, :, :3 * A_WIDTH].astype(BF16)
    o["w_b"] = w_in[:, :, 3 * A_WIDTH:3 * A_WIDTH + 3 * B_WIDTH].astype(BF16)
    c0 = 3 * A_WIDTH + 3 * B_WIDTH
    c1 = c0 + C_Q_RANK + C_KV_RANK
    zeros = lambda k: jnp.zeros((nl, d, k), F32)
    o["w_c"] = jnp.concatenate([w_in[:, :, c0:c1], zeros(C_NOPE), w_in[:, :, c1:c1 + C_ROPE],
                                zeros(LANES - C_NOPE - C_ROPE)], axis=-1).astype(BF16)
    o["w_g"] = w_in[:, :, c1 + C_ROPE:].astype(BF16)
    wq = p["w_q_b"].reshape(nl, C_Q_RANK, C_HEADS, C_NOPE + C_ROPE)
    wq = jnp.pad(wq, ((0, 0), (0, 0), (0, 0), (0, LANES - C_NOPE - C_ROPE)))
    o["w_q_b"] = wq.reshape(nl, C_Q_RANK, C_HEADS * LANES).astype(BF16)
    wkv = p["w_kv_b"].reshape(nl, C_KV_RANK, C_HEADS, C_NOPE + C_VDIM)
    wk = jnp.pad(wkv[..., :C_NOPE], ((0, 0), (0, 0), (0, 0), (0, LANES - C_NOPE)))
    o["w_kv_b"] = jnp.concatenate([wk.reshape(nl, C_KV_RANK, C_HEADS * LANES),
                                   wkv[..., C_NOPE:].reshape(nl, C_KV_RANK, C_WIDTH)], axis=-1).astype(BF16)
    for name in ("w_br_a", "w_br_b", "w_br_c", "w_out"):
        o[name] = p[name].astype(BF16)
    o["w_gate_up"], o["w_down"] = _moe_weight_prep(p["w_gate_up"], p["w_down"])
    o["b_gate_up"] = _regroup_gate_up(p["b_gate_up"]).reshape(-1, 1, p["b_gate_up"].shape[-1])
    o["b_down"] = p["b_down"].reshape(-1, 1, p["b_down"].shape[-1])
    o["w_router"] = jnp.pad(p["w_router"], ((0, 0), (0, 0), (0, LANES - N_EXPERTS)))
    o["b_router"] = jnp.pad(p["b_router"], ((0, 0), (0, LANES - N_EXPERTS)))[:, None, :]
    return o


@jax.jit
def _trunk(p):
    x, ctx = p["x"], p["ctx"]
    bsz, n, d = x.shape
    n_ctx = ctx.shape[1]
    depth = p["w_mod"].shape[0]
    rows = _Rows(bsz, n, n_ctx)
    w = _prep_weights(p)
    rope_a = _rope_table(n, A_HEAD_DIM, 0, A_HEAD_DIM)
    rope_c = _rope_table(n, C_ROPE, C_NOPE, LANES)
    scale_a = A_HEAD_DIM ** -0.5 * LOG2E
    scale_b = B_HEAD_DIM ** -0.5 * LOG2E
    scale_c = (C_NOPE + C_ROPE) ** -0.5 * LOG2E
    c8 = jnp.zeros((8, d), F32).at[:bsz].set(p["c"]).at[bsz].set(p["c_ctx"])
    x_all = jnp.concatenate([x.reshape(bsz * n, d), ctx.reshape(bsz * n_ctx, d)], axis=0)
    dummy_lam = jnp.zeros((4, A_HEAD_DIM), F32)
    dummy_g = jnp.ones((1, LANES), F32)
    out = None
    for l in range(depth):
        last = l == depth - 1
        lam_init = 0.8 - 0.6 * math.exp(-0.3 * l)
        mod3 = _mod_call(c8, p["w_mod"][l], p["b_mod"][l]).reshape(8 * N_MOD, 1, d)
        g_mix = p["g_mix"][l][None, :]
        qa, ka, va = _inproj_qkv_call(rows, x_all, g_mix, mod3, w["w_a"][l], rope_a, scale_a)
        qb, kb, vb = _inproj_qkv_call(rows, x_all, g_mix, mod3, w["w_b"][l], None, scale_b)
        qc, kc, vc = _inproj_mla_call(rows, x_all, g_mix, mod3, w["w_c"][l], p["g_q_a"][l][None, :],
                                      w["w_q_b"][l], p["g_kv_a"][l][None, :], w["w_kv_b"][l], rope_c, scale_c)
        gates = _inproj_gates_call(rows, x_all, g_mix, mod3, w["w_g"][l])
        lamv = jnp.stack([p["lam_q1"][l], p["lam_k1"][l], p["lam_q2"][l], p["lam_k2"][l]])
        gs = p["g_subln"][l][None, :]
        attn = functools.partial(_pair_attention_call, rows)
        ya = attn(qa, ka, va, lamv, gs, wq=LANES, mode="diff", lam_init=lam_init, ctx_queries=False)
        yb = _nbr_attention_call(rows, qb, kb, vb, p["rpb"][l])
        yc = attn(qc, kc, vc, dummy_lam, dummy_g, wq=2 * LANES, mode="pair", lam_init=0.0, ctx_queries=False)
        if not last:
            ya_c = attn(qa, ka, va, lamv, gs, wq=LANES, mode="diff", lam_init=lam_init, ctx_queries=True)
            yb_c = attn(qb, kb, vb, dummy_lam, dummy_g, wq=LANES, mode="pair", lam_init=0.0, ctx_queries=True)
            yc_c = attn(qc, kc, vc, dummy_lam, dummy_g, wq=2 * LANES, mode="pair", lam_init=0.0, ctx_queries=True)
            ya, yb, yc = (jnp.concatenate(pair, axis=0) for pair in ((ya, ya_c), (yb, yb_c), (yc, yc_c)))
        n_tiles = rows.lat_tiles if last else rows.all_tiles
        x1, h2, ri, rw, rk, cnt = _merge_call(rows, n_tiles, ya, yb, yc, gates, x_all, w["w_br_a"][l], w["w_br_b"][l],
                                     w["w_br_c"][l], w["w_out"][l], mod3, p["g_ffn"][l][None, :],
                                     w["w_router"][l], w["b_router"][l])
        dest, blk_e, n_used, blk_valid, n_rows = _route(ri[:, :TOP_K], rk[:, :TOP_K],
                                                        cnt[0, :N_EXPERTS].astype(jnp.int32))
        t_rows = n_tiles * TM
        n_pieces = h2.shape[0]
        piece_off = (jnp.arange(n_pieces, dtype=jnp.int32) * n_rows)[:, None]
        xg = _sc_scatter_rows(h2.reshape(n_pieces * t_rows, LANES),
                              [(piece_off + dest[None, :, j]).reshape(1, -1) for j in range(TOP_K)],
                              n_pieces * n_rows).reshape(n_pieces, n_rows, LANES)
        y = _moe_call(xg, blk_e + l * N_EXPERTS, n_used, blk_valid, w["w_gate_up"], w["b_gate_up"],
                      w["w_down"], w["b_down"])
        slab_off = (jnp.arange(Y_SLABS, dtype=jnp.int32) * n_rows)[:, None, None]
        yg = _sc_gather_rows(y.reshape(Y_SLABS * n_rows, d // Y_SLABS),
                             (slab_off + dest.T[None]).reshape(1, -1)).reshape(Y_SLABS, TOP_K, t_rows, d // Y_SLABS)
        x_new = _combine_call(rows, n_tiles, x1, yg, rw, mod3, p["g_final"][None, :], last)
        if last:
            out = x_new.reshape(bsz, n, d)
        else:
            x_all = x_new
    return out


def kernel(x, c, ctx, c_ctx, w_mod, b_mod, g_mix, w_in, lam_q1, lam_k1, lam_q2, lam_k2, g_subln, rpb, g_q_a, w_q_b, g_kv_a, w_kv_b, w_br_a, w_br_b, w_br_c, w_out, g_ffn, w_router, b_router, w_gate_up, b_gate_up, w_down, b_down, g_final):
    return _trunk(dict(x=x, c=c, ctx=ctx, c_ctx=c_ctx, w_mod=w_mod, b_mod=b_mod, g_mix=g_mix, w_in=w_in,
                       lam_q1=lam_q1, lam_k1=lam_k1, lam_q2=lam_q2, lam_k2=lam_k2, g_subln=g_subln, rpb=rpb,
                       g_q_a=g_q_a, w_q_b=w_q_b, g_kv_a=g_kv_a, w_kv_b=w_kv_b, w_br_a=w_br_a, w_br_b=w_br_b,
                       w_br_c=w_br_c, w_out=w_out, g_ffn=g_ffn, w_router=w_router, b_router=b_router,
                       w_gate_up=w_gate_up, b_gate_up=b_gate_up, w_down=w_down, b_down=b_down, g_final=g_final))
```

```python
import functools
import math

import jax
import jax.numpy as jnp
from jax import lax
from jax.experimental import pallas as pl
from jax.experimental.pallas import tpu as pltpu
from jax.experimental.pallas import tpu_sc as plsc

GRID_W = 64
A_HEADS = 4
A_HEAD_DIM = 64
A_WIDTH = A_HEADS * 2 * A_HEAD_DIM
B_HEADS = 8
B_HEAD_DIM = 64
B_WIDTH = B_HEADS * B_HEAD_DIM
NA_ROWS = 8
NA_COLS = 16
C_HEADS = 8
C_NOPE = 64
C_ROPE = 32
C_VDIM = 64
C_Q_RANK = 384
C_KV_RANK = 256
C_WIDTH = C_HEADS * C_VDIM
N_BRANCH = 3
N_EXPERTS = 32
TOP_K = 4
SWIGLU_LIMIT = 7.0
SWIGLU_ALPHA = 1.702
N_MOD = 6
ROPE_BASE = 10000.0
EPS = 1e-6
NEG_INF = -1e30
LOG2E = 1.4426950408889634

LANES = 128
VMEM_LIMIT = 56 * 1024 * 1024
TM = 512
TQ = 512
TK = 2048
MOE_TB = 512
Y_SLABS = 4
SC_WINDOW = 128
NB_GROUP = 4
NB_GROUPS_PER_STEP = 16

BF16 = jnp.bfloat16
F32 = jnp.float32


def _cparams(sem):
    return pltpu.CompilerParams(dimension_semantics=sem, vmem_limit_bytes=VMEM_LIMIT)


def _dot(a, b):
    return jnp.dot(a, b, preferred_element_type=F32)


def _dot_nt(a, b):
    return lax.dot_general(a, b, (((1,), (1,)), ((), ())), preferred_element_type=F32)


def _split_bf16(a):
    hi = a.astype(BF16)
    lo = (a - hi.astype(F32)).astype(BF16)
    return hi, lo


def _dot_split(a, b):
    a_hi, a_lo = _split_bf16(a)
    b_hi, b_lo = _split_bf16(b)
    return _dot(a_hi, b_hi) + _dot(a_hi, b_lo) + _dot(a_lo, b_hi)


def _rms(x):
    return x * lax.rsqrt(jnp.mean(x * x, axis=-1, keepdims=True) + EPS)


def _modulated(x_ref, g_ref, sh_ref, sc_ref):
    return (_rms(x_ref[...]) * g_ref[...]) * (1.0 + sc_ref[0]) + sh_ref[0]


def _rope(blk, cos, sin_signed, shift):
    lane = lax.broadcasted_iota(jnp.int32, blk.shape, 1)
    first = ((lane // shift) % 2) == 0
    partner = jnp.where(first, pltpu.roll(blk, LANES - shift, 1), pltpu.roll(blk, shift, 1))
    return blk * cos + partner * sin_signed


def _mod_kernel(c_ref, w_ref, b_ref, o_ref):
    cc = c_ref[...]
    a = cc * jax.nn.sigmoid(cc)
    o_ref[...] = _dot_split(a, w_ref[...]) + b_ref[...]


def _mod_call(c8, w_mod, b_mod):
    d = c8.shape[1]
    n_out = w_mod.shape[1]
    return pl.pallas_call(
        _mod_kernel,
        grid=(n_out // d,),
        in_specs=[pl.BlockSpec((8, d), lambda j: (0, 0)),
                  pl.BlockSpec((d, d), lambda j: (0, j)),
                  pl.BlockSpec((1, d), lambda j: (0, j))],
        out_specs=pl.BlockSpec((8, d), lambda j: (0, j)),
        out_shape=jax.ShapeDtypeStruct((8, n_out), F32),
        compiler_params=_cparams(("arbitrary",)),
        name="mod",
    )(c8, w_mod, b_mod.reshape(1, n_out))


class _Rows:
    def __init__(self, bsz, n, n_ctx):
        assert n % TM == 0 and (bsz * n_ctx) % TM == 0
        self.bsz, self.n, self.n_ctx = bsz, n, n_ctx
        self.t_lat = bsz * n
        self.t_all = bsz * n + bsz * n_ctx
        self.tiles_per_batch = n // TM
        self.lat_tiles = self.t_lat // TM
        self.all_tiles = self.t_all // TM

    def mod_spec(self, which, d):
        tpb, bsz = self.tiles_per_batch, self.bsz
        return pl.BlockSpec((1, 1, d), lambda i, *_: (jnp.minimum(i // tpb, bsz) * N_MOD + which, 0, 0))

    def rope_spec(self):
        tpb, lat = self.tiles_per_batch, self.lat_tiles
        return pl.BlockSpec((TM, LANES), lambda i, *_: (jnp.where(i < lat, i % tpb, tpb), 0))


def _inproj_qkv_kernel(*refs, rope, qscale):
    if rope:
        x_ref, g_ref, sh_ref, sc_ref, w_ref, cos_ref, sin_ref, q_ref, k_ref, v_ref = refs
    else:
        x_ref, g_ref, sh_ref, sc_ref, w_ref, q_ref, k_ref, v_ref = refs
    h = _modulated(x_ref, g_ref, sh_ref, sc_ref).astype(BF16)
    t = _dot(h, w_ref[...])
    outs = (q_ref, k_ref, v_ref)
    per = q_ref.shape[1] // LANES
    for j in range(3 * per):
        blk = t[:, j * LANES:(j + 1) * LANES]
        if rope and j < 2 * per:
            blk = _rope(blk, cos_ref[...], sin_ref[...], A_HEAD_DIM // 4)
        if j < per:
            blk = blk * qscale
        outs[j // per][:, (j % per) * LANES:(j % per + 1) * LANES] = blk.astype(BF16)


def _inproj_qkv_call(rows, x_all, g, mod3, w, tables, qscale):
    d = x_all.shape[1]
    width = w.shape[1] // 3
    rope = tables is not None
    in_specs = [pl.BlockSpec((TM, d), lambda i: (i, 0)),
                pl.BlockSpec((1, d), lambda i: (0, 0)),
                rows.mod_spec(0, d), rows.mod_spec(1, d),
                pl.BlockSpec((d, 3 * width), lambda i: (0, 0))]
    args = [x_all, g, mod3, mod3, w]
    if rope:
        in_specs += [rows.rope_spec(), rows.rope_spec()]
        args += list(tables)
    out = jax.ShapeDtypeStruct((rows.t_all, width), BF16)
    return pl.pallas_call(
        functools.partial(_inproj_qkv_kernel, rope=rope, qscale=qscale),
        grid=(rows.all_tiles,),
        in_specs=in_specs,
        out_specs=[pl.BlockSpec((TM, width), lambda i: (i, 0))] * 3,
        out_shape=[out] * 3,
        compiler_params=_cparams(("parallel",)),
        name="inproj_qkv_rope" if rope else "inproj_qkv",
    )(*args)


def _inproj_gates_kernel(x_ref, g_ref, sh_ref, sc_ref, w_ref, o_ref):
    h = _modulated(x_ref, g_ref, sh_ref, sc_ref).astype(BF16)
    d = x_ref.shape[1]
    for j in range(o_ref.shape[1] // d):
        o_ref[:, j * d:(j + 1) * d] = jax.nn.sigmoid(_dot(h, w_ref[:, j * d:(j + 1) * d])).astype(BF16)


def _inproj_gates_call(rows, x_all, g, mod3, w):
    d = x_all.shape[1]
    n_out = w.shape[1]
    return pl.pallas_call(
        _inproj_gates_kernel,
        grid=(rows.all_tiles,),
        in_specs=[pl.BlockSpec((TM, d), lambda i: (i, 0)),
                  pl.BlockSpec((1, d), lambda i: (0, 0)),
                  rows.mod_spec(0, d), rows.mod_spec(1, d),
                  pl.BlockSpec((d, n_out), lambda i: (0, 0))],
        out_specs=pl.BlockSpec((TM, n_out), lambda i: (i, 0)),
        out_shape=jax.ShapeDtypeStruct((rows.t_all, n_out), BF16),
        compiler_params=_cparams(("parallel",)),
        name="inproj_gates",
    )(x_all, g, mod3, mod3, w)


def _inproj_mla_kernel(x_ref, g_ref, sh_ref, sc_ref, w_ref, gq_ref, wq_ref, gkv_ref, wkv_ref,
                       cos_ref, sin_ref, q_ref, k_ref, v_ref, *, qscale):
    h = _modulated(x_ref, g_ref, sh_ref, sc_ref).astype(BF16)
    t = _dot(h, w_ref[...])
    cos, sin = cos_ref[...], sin_ref[...]
    shift = C_ROPE // 4
    cq = (_rms(t[:, :C_Q_RANK]) * gq_ref[...]).astype(BF16)
    q = _dot(cq, wq_ref[...])
    for j in range(C_HEADS):
        blk = _rope(q[:, j * LANES:(j + 1) * LANES], cos, sin, shift) * qscale
        q_ref[:, j * LANES:(j + 1) * LANES] = blk.astype(BF16)
    ckv = (_rms(t[:, C_Q_RANK:C_Q_RANK + C_KV_RANK]) * gkv_ref[...]).astype(BF16)
    kv = _dot(ckv, wkv_ref[...])
    pe = _rope(t[:, C_Q_RANK + C_KV_RANK:], cos, sin, shift)
    for j in range(C_HEADS):
        k_ref[:, j * LANES:(j + 1) * LANES] = (kv[:, j * LANES:(j + 1) * LANES] + pe).astype(BF16)
    v_ref[...] = kv[:, C_HEADS * LANES:].astype(BF16)


def _inproj_mla_call(rows, x_all, g, mod3, w, gq, wq, gkv, wkv, tables, qscale):
    d = x_all.shape[1]
    full = lambda a: pl.BlockSpec(a.shape, lambda i: (0,) * a.ndim)
    qk = jax.ShapeDtypeStruct((rows.t_all, C_HEADS * LANES), BF16)
    return pl.pallas_call(
        functools.partial(_inproj_mla_kernel, qscale=qscale),
        grid=(rows.all_tiles,),
        in_specs=[pl.BlockSpec((TM, d), lambda i: (i, 0)), full(g),
                  rows.mod_spec(0, d), rows.mod_spec(1, d),
                  full(w), full(gq), full(wq), full(gkv), full(wkv),
                  rows.rope_spec(), rows.rope_spec()],
        out_specs=[pl.BlockSpec((TM, C_HEADS * LANES), lambda i: (i, 0)),
                   pl.BlockSpec((TM, C_HEADS * LANES), lambda i: (i, 0)),
                   pl.BlockSpec((TM, C_WIDTH), lambda i: (i, 0))],
        out_shape=[qk, qk, jax.ShapeDtypeStruct((rows.t_all, C_WIDTH), BF16)],
        compiler_params=_cparams(("parallel",)),
        name="inproj_mla",
    )(x_all, g, mod3, mod3, w, gq, wq, gkv, wkv, *tables)


def _pair_attention_kernel(*refs, n_lat_chunks, tk, mode, lam_init):
    if n_lat_chunks:
        q_ref, kl_ref, vl_ref, kc_ref, vc_ref, lam_ref, gs_ref, o_ref = refs
    else:
        q_ref, kc_ref, vc_ref, lam_ref, gs_ref, o_ref = refs
    q = q_ref[...]
    tq, wq = q.shape
    lane = lax.broadcasted_iota(jnp.int32, q.shape, 1)
    lower = lane < wq // 2
    zero = jnp.zeros_like(q)
    qs = jnp.concatenate([jnp.where(lower, q, zero), jnp.where(lower, zero, q)], axis=0)

    def step(kc, vc, carry):
        m, l, acc = carry
        s = _dot_nt(qs, kc)
        m_new = jnp.maximum(m, jnp.max(s, axis=-1, keepdims=True))
        p = jnp.exp2(s - m_new)
        alpha = jnp.exp2(m - m_new)
        l = alpha * l + jnp.sum(p, axis=-1, keepdims=True)
        acc = alpha * acc + _dot(p.astype(BF16), vc)
        return m_new, l, acc

    carry = (jnp.full((2 * tq, 1), NEG_INF, F32), jnp.zeros((2 * tq, 1), F32),
             jnp.zeros((2 * tq, LANES), F32))
    for i in range(n_lat_chunks):
        carry = step(kl_ref[i * tk:(i + 1) * tk, :], vl_ref[i * tk:(i + 1) * tk, :], carry)
    m, l, acc = step(kc_ref[...], vc_ref[...], carry)
    o = acc / l
    o0, o1 = o[:tq], o[tq:]
    if mode == "diff":
        lv = lam_ref[...]
        lam = (jnp.exp(jnp.sum(lv[0:1] * lv[1:2], axis=-1, keepdims=True))
               - jnp.exp(jnp.sum(lv[2:3] * lv[3:4], axis=-1, keepdims=True)) + lam_init)
        out = (_rms(o0 - lam * o1) * gs_ref[...]) * (1.0 - lam_init)
    else:
        out = jnp.where(lax.broadcasted_iota(jnp.int32, o0.shape, 1) < LANES // 2, o0, o1)
    o_ref[...] = out.astype(BF16)


def _pair_attention_call(rows, q, k, v, lamv, gs, *, wq, mode, lam_init, ctx_queries):
    bsz, n, n_ctx = rows.bsz, rows.n, rows.n_ctx
    n_pairs = v.shape[1] // LANES
    ctx_blk0 = rows.t_lat // n_ctx
    kc_spec = pl.BlockSpec((n_ctx, wq), lambda b, h, i: (ctx_blk0 + b, h))
    vc_spec = pl.BlockSpec((n_ctx, LANES), lambda b, h, i: (ctx_blk0 + b, h))
    small = [pl.BlockSpec(lamv.shape, lambda b, h, i: (0, 0)), pl.BlockSpec(gs.shape, lambda b, h, i: (0, 0))]
    if ctx_queries:
        tq, n_q, n_chunks, tk = n_ctx, 1, 0, 0
        q_spec = pl.BlockSpec((tq, wq), lambda b, h, i: (ctx_blk0 + b, h))
        o_spec = pl.BlockSpec((tq, LANES), lambda b, h, i: (b, h))
        in_specs = [q_spec, kc_spec, vc_spec] + small
        args = (q, k, v, lamv, gs)
        out_rows = bsz * n_ctx
    else:
        tk = min(TK, n)
        tq = min(TQ, n)
        assert n % tq == 0 and n % tk == 0
        n_q, n_chunks = n // tq, n // tk
        q_spec = pl.BlockSpec((tq, wq), lambda b, h, i: (b * n_q + i, h))
        o_spec = pl.BlockSpec((tq, LANES), lambda b, h, i: (b * n_q + i, h))
        in_specs = [q_spec,
                    pl.BlockSpec((n, wq), lambda b, h, i: (b, h)),
                    pl.BlockSpec((n, LANES), lambda b, h, i: (b, h)),
                    kc_spec, vc_spec] + small
        args = (q, k, v, k, v, lamv, gs)
        out_rows = rows.t_lat
    return pl.pallas_call(
        functools.partial(_pair_attention_kernel, n_lat_chunks=n_chunks, tk=tk, mode=mode, lam_init=lam_init),
        grid=(bsz, n_pairs, n_q),
        in_specs=in_specs,
        out_specs=o_spec,
        out_shape=jax.ShapeDtypeStruct((out_rows, n_pairs * LANES), BF16),
        compiler_params=_cparams(("parallel", "parallel", "arbitrary")),
        name=f"attn_{mode}_{'ctx' if ctx_queries else 'lat'}_{wq}",
    )(*args)


def _nbr_groups(n_rows):
    half = NA_ROWS // 2
    span = NA_ROWS + NB_GROUP - 1
    starts, classes, patterns = [], [], []
    for r0 in range(0, n_rows, NB_GROUP):
        rs = [min(max(r - half, 0), n_rows - NA_ROWS) for r in range(r0, r0 + NB_GROUP)]
        us = min(rs[0], n_rows - span)
        pat = (tuple(x - us for x in rs), tuple(r - us for r in range(r0, r0 + NB_GROUP)))
        if pat not in patterns:
            patterns.append(pat)
        starts.append(us)
        classes.append(patterns.index(pat))
    return starts, classes, patterns


def _nbr_bias_table(rpb, patterns):
    span = NA_ROWS + NB_GROUP - 1
    c_idx = jnp.arange(GRID_W, dtype=jnp.int32)
    col_start = jnp.clip(c_idx - NA_COLS // 2, 0, GRID_W - NA_COLS)
    col_mask = (c_idx[None, :] >= col_start[:, None]) & (c_idx[None, :] < col_start[:, None] + NA_COLS)
    col_bias_idx = jnp.clip(c_idx[None, :] - c_idx[:, None] + NA_COLS - 1, 0, 2 * NA_COLS - 2)
    rpb_c = rpb[:, :, col_bias_idx] * LOG2E
    u = jnp.arange(span)
    tabs = []
    for rs_rel, r_rel in patterns:
        per_row = []
        for a in range(NB_GROUP):
            valid = (u >= rs_rel[a]) & (u < rs_rel[a] + NA_ROWS)
            di = jnp.clip(u - r_rel[a] + NA_ROWS - 1, 0, 2 * NA_ROWS - 2)
            bias = jnp.transpose(rpb_c[:, di], (0, 2, 1, 3))
            ok = valid[None, None, :, None] & col_mask[None, :, None, :]
            per_row.append(jnp.where(ok, bias, NEG_INF).reshape(rpb.shape[0], GRID_W, span * GRID_W))
        tabs.append(jnp.concatenate(per_row, axis=1))
    return jnp.stack(tabs).astype(F32)


def _nbr_attention_kernel(us_ref, cls_ref, q_ref, kl_ref, vl_ref, kc_ref, vc_ref, bias_ref, o_ref, *, groups_per_step):
    kc, vc = kc_ref[...], vc_ref[...]
    rows_q = NB_GROUP * GRID_W
    win = (NA_ROWS + NB_GROUP - 1) * GRID_W

    def one_group(j, _):
        g = pl.program_id(2) * groups_per_step + j
        q = q_ref[pl.ds(pl.multiple_of(j * rows_q, rows_q), rows_q), :]
        lane = lax.broadcasted_iota(jnp.int32, q.shape, 1)
        lower = lane < LANES // 2
        zero = jnp.zeros_like(q)
        qs = jnp.concatenate([jnp.where(lower, q, zero), jnp.where(lower, zero, q)], axis=0)
        off = pl.multiple_of(us_ref[g] * GRID_W, GRID_W)
        kw = kl_ref[pl.ds(off, win), :]
        vw = vl_ref[pl.ds(off, win), :]
        bias = bias_ref[cls_ref[g]].reshape(2 * rows_q, win)
        s = jnp.where(bias > 0.5 * NEG_INF, _dot_nt(qs, kw) + bias, NEG_INF)
        s_ctx = _dot_nt(qs, kc)
        m = jnp.maximum(jnp.max(s, axis=-1, keepdims=True), jnp.max(s_ctx, axis=-1, keepdims=True))
        p = jnp.exp2(s - m)
        p_ctx = jnp.exp2(s_ctx - m)
        l = jnp.sum(p, axis=-1, keepdims=True) + jnp.sum(p_ctx, axis=-1, keepdims=True)
        o = (_dot(p.astype(BF16), vw) + _dot(p_ctx.astype(BF16), vc)) / l
        out = jnp.where(lax.broadcasted_iota(jnp.int32, (rows_q, LANES), 1) < LANES // 2,
                        o[:rows_q], o[rows_q:])
        o_ref[pl.ds(pl.multiple_of(j * rows_q, rows_q), rows_q), :] = out.astype(BF16)
        return 0

    lax.fori_loop(0, groups_per_step, one_group, 0, unroll=True)


def _nbr_attention_call(rows, q, k, v, rpb):
    bsz, n, n_ctx = rows.bsz, rows.n, rows.n_ctx
    n_rows = n // GRID_W
    groups_per_step = min(NB_GROUPS_PER_STEP, n_rows // NB_GROUP)
    rows_per_step = NB_GROUP * groups_per_step
    assert n_rows % rows_per_step == 0 and n_rows >= NA_ROWS + NB_GROUP - 1
    starts, classes, patterns = _nbr_groups(n_rows)
    bias = _nbr_bias_table(rpb, patterns)
    steps = n_rows // rows_per_step
    n_pairs = B_HEADS // 2
    ctx_blk0 = rows.t_lat // n_ctx
    tq = rows_per_step * GRID_W
    grid_spec = pltpu.PrefetchScalarGridSpec(
        num_scalar_prefetch=2,
        grid=(bsz, n_pairs, steps),
        in_specs=[pl.BlockSpec((tq, LANES), lambda b, h, i, us, cl: (b * steps + i, h)),
                  pl.BlockSpec((n, LANES), lambda b, h, i, us, cl: (b, h)),
                  pl.BlockSpec((n, LANES), lambda b, h, i, us, cl: (b, h)),
                  pl.BlockSpec((n_ctx, LANES), lambda b, h, i, us, cl: (ctx_blk0 + b, h)),
                  pl.BlockSpec((n_ctx, LANES), lambda b, h, i, us, cl: (ctx_blk0 + b, h)),
                  pl.BlockSpec((bias.shape[0], 2) + bias.shape[2:], lambda b, h, i, us, cl: (0, h, 0, 0))],
        out_specs=pl.BlockSpec((tq, LANES), lambda b, h, i, us, cl: (b * steps + i, h)),
    )
    return pl.pallas_call(
        functools.partial(_nbr_attention_kernel, groups_per_step=groups_per_step),
        grid_spec=grid_spec,
        out_shape=jax.ShapeDtypeStruct((rows.t_lat, B_WIDTH), BF16),
        compiler_params=_cparams(("parallel", "parallel", "arbitrary")),
        name="attn_nbr",
    )(jnp.asarray(starts, jnp.int32), jnp.asarray(classes, jnp.int32), q, k, v, k, v, bias)


def _merge_kernel(ya_ref, yb_ref, yc_ref, g_ref, x_ref, wa_ref, wb_ref, wc_ref, wo_ref,
                  gt1_ref, gffn_ref, sh2_ref, sc2_ref, wr_ref, br_ref,
                  x1_ref, h2_ref, ri_ref, rw_ref, rk_ref, cnt_ref):
    d = x_ref.shape[1]

    @pl.when(pl.program_id(0) == 0)
    def _():
        cnt_ref[...] = jnp.zeros_like(cnt_ref)

    g = g_ref[...].astype(F32)
    m = (g[:, :d] * _dot(ya_ref[...], wa_ref[...])
         + g[:, d:2 * d] * _dot(yb_ref[...], wb_ref[...])
         + g[:, 2 * d:] * _dot(yc_ref[...], wc_ref[...]))
    y = _dot(m.astype(BF16), wo_ref[...])
    x1 = x_ref[...] + gt1_ref[0] * y
    x1_ref[...] = x1
    h2 = (_rms(x1) * gffn_ref[...]) * (1.0 + sc2_ref[0]) + sh2_ref[0]
    hb = lax.bitcast_convert_type(h2.astype(BF16).astype(F32), jnp.uint32)
    words = hb[:, :d // 2] | (hb[:, d // 2:] >> 16)
    for c in range(h2_ref.shape[0]):
        h2_ref[c] = words[:, c * LANES:(c + 1) * LANES]
    logits = _dot_split(h2, wr_ref[...]) + br_ref[...]
    lane = lax.broadcasted_iota(jnp.int32, logits.shape, 1)
    lane_f = lane.astype(F32)
    work = jnp.where(lane < N_EXPERTS, logits, NEG_INF)
    idx_out = jnp.zeros(logits.shape, F32)
    val_out = jnp.zeros(logits.shape, F32)
    onehot = jnp.zeros(logits.shape, F32)
    top = None
    den = jnp.zeros((logits.shape[0], 1), F32)
    idxs = []
    for j in range(TOP_K):
        mx = jnp.max(work, axis=-1, keepdims=True)
        idx = jnp.min(jnp.where(work == mx, lane_f, float(LANES)), axis=-1, keepdims=True)
        if top is None:
            top = mx
        e = jnp.exp(mx - top)
        den = den + e
        idx_out = jnp.where(lane == j, idx, idx_out)
        val_out = jnp.where(lane == j, e, val_out)
        picked = lane_f == idx
        onehot = jnp.where(picked, 1.0, onehot)
        work = jnp.where(picked, NEG_INF, work)
        idxs.append(idx)
    ri_ref[...] = idx_out.astype(jnp.int32)
    rw_ref[...] = val_out / den
    tm = logits.shape[0]
    earlier = (lax.broadcasted_iota(jnp.int32, (tm, tm), 1) < lax.broadcasted_iota(jnp.int32, (tm, tm), 0))
    before = _dot(earlier.astype(BF16), onehot.astype(BF16)) + cnt_ref[0:1, :]
    rank_out = jnp.zeros(logits.shape, F32)
    for j in range(TOP_K):
        rank_j = jnp.sum(jnp.where(lane_f == idxs[j], before, 0.0), axis=-1, keepdims=True)
        rank_out = jnp.where(lane == j, rank_j, rank_out)
    rk_ref[...] = rank_out.astype(jnp.int32)
    cnt_ref[...] = cnt_ref[...] + jnp.sum(onehot, axis=0, keepdims=True)


def _merge_call(rows, n_tiles, ya, yb, yc, gates, x_all, wa, wb, wc, wo, mod3, gffn, wr, br):
    d = x_all.shape[1]
    t = n_tiles * TM
    full = lambda a: pl.BlockSpec(a.shape, lambda i: (0,) * a.ndim)
    row = lambda w: pl.BlockSpec((TM, w), lambda i: (i, 0))
    return pl.pallas_call(
        _merge_kernel,
        grid=(n_tiles,),
        in_specs=[row(ya.shape[1]), row(yb.shape[1]), row(yc.shape[1]), row(gates.shape[1]), row(d),
                  full(wa), full(wb), full(wc), full(wo),
                  rows.mod_spec(2, d), full(gffn), rows.mod_spec(3, d), rows.mod_spec(4, d),
                  full(wr), full(br)],
        out_specs=[row(d), pl.BlockSpec((d // 2 // LANES, TM, LANES), lambda i: (0, i, 0)),
                   row(LANES), row(LANES), row(LANES),
                   pl.BlockSpec((8, LANES), lambda i: (0, 0))],
        out_shape=[jax.ShapeDtypeStruct((t, d), F32), jax.ShapeDtypeStruct((d // 2 // LANES, t, LANES), jnp.uint32),
                   jax.ShapeDtypeStruct((t, LANES), jnp.int32), jax.ShapeDtypeStruct((t, LANES), F32),
                   jax.ShapeDtypeStruct((t, LANES), jnp.int32), jax.ShapeDtypeStruct((8, LANES), F32)],
        compiler_params=_cparams(("arbitrary",)),
        name="merge_router",
    )(ya, yb, yc, gates, x_all, wa, wb, wc, wo, mod3, gffn, mod3, mod3, wr, br)


def _moe_kernel(blk_e_ref, n_used_ref, valid_ref, x_ref, wgu_ref, bgu_ref, wd_ref, bd_ref, o_ref, wgu_s, wd_s):
    i = pl.program_id(0)
    new_expert = (i == 0) | (blk_e_ref[i] != blk_e_ref[jnp.maximum(i - 1, 0)])

    @pl.when(new_expert & (i < n_used_ref[0]))
    def _():
        grp = 2 * LANES
        out_col = lax.broadcasted_iota(jnp.int32, (grp, grp), 1)
        src_col = jnp.where(out_col < LANES, 2 * out_col, 2 * (out_col - LANES) + 1)
        regroup = (lax.broadcasted_iota(jnp.int32, (grp, grp), 0) == src_col).astype(BF16)
        for j in range(wgu_s.shape[1] // grp):
            cols = slice(j * grp, (j + 1) * grp)
            wgu_s[:, cols] = _dot(wgu_ref[0, :, cols].astype(BF16), regroup).astype(BF16)
        wd_s[...] = wd_ref[0].astype(BF16)

    @pl.when(i < n_used_ref[0])
    def _():
        words = [x_ref[c] for c in range(x_ref.shape[0])]
        hi = [lax.bitcast_convert_type(w & jnp.uint32(0xFFFF0000), F32) for w in words]
        lo = [lax.bitcast_convert_type(w << 16, F32) for w in words]
        x = jnp.concatenate(hi + lo, axis=1)
        row = lax.broadcasted_iota(jnp.int32, x.shape, 0)
        x = jnp.where(row < valid_ref[i], x, 0.0).astype(BF16)
        gu = _dot(x, wgu_s[...]) + bgu_ref[0]
        acts = []
        for j in range(gu.shape[1] // (2 * LANES)):
            gate = jnp.minimum(gu[:, 2 * j * LANES:(2 * j + 1) * LANES], SWIGLU_LIMIT)
            up = jnp.clip(gu[:, (2 * j + 1) * LANES:(2 * j + 2) * LANES], -SWIGLU_LIMIT, SWIGLU_LIMIT)
            acts.append((up + 1.0) * (gate * jax.nn.sigmoid(SWIGLU_ALPHA * gate)))
        act = jnp.concatenate(acts, axis=1).astype(BF16)
        y = _dot(act, wd_s[...]) + bd_ref[0]
        width = o_ref.shape[2]
        for c in range(o_ref.shape[0]):
            o_ref[c] = y[:, c * width:(c + 1) * width]

    @pl.when(i >= n_used_ref[0])
    def _():
        o_ref[...] = jnp.zeros_like(o_ref)


def _regroup_gate_up(v):
    lead = v.shape[:-1]
    return jnp.swapaxes(v.reshape(lead + (-1, LANES, 2)), -1, -2).reshape(lead + (-1,))


def _moe_call(xg, blk_e, n_used, blk_valid, wgu, bgu, wd, bd):
    n_pieces, n_rows, _ = xg.shape
    d, f2, f = wgu.shape[1], wgu.shape[2], wd.shape[1]
    n_blk = n_rows // MOE_TB
    last = lambda i, be, nu, bv: jnp.minimum(i, nu[0] - 1)
    grid_spec = pltpu.PrefetchScalarGridSpec(
        num_scalar_prefetch=3,
        grid=(n_blk,),
        in_specs=[pl.BlockSpec((n_pieces, MOE_TB, LANES), lambda i, be, nu, bv: (0, last(i, be, nu, bv), 0)),
                  pl.BlockSpec((1, d, f2), lambda i, be, nu, bv: (be[i], 0, 0)),
                  pl.BlockSpec((1, 1, f2), lambda i, be, nu, bv: (be[i], 0, 0)),
                  pl.BlockSpec((1, f, d), lambda i, be, nu, bv: (be[i], 0, 0)),
                  pl.BlockSpec((1, 1, d), lambda i, be, nu, bv: (be[i], 0, 0))],
        out_specs=pl.BlockSpec((Y_SLABS, MOE_TB, d // Y_SLABS), lambda i, be, nu, bv: (0, i, 0)),
        scratch_shapes=[pltpu.VMEM((d, f2), BF16), pltpu.VMEM((f, d), BF16)],
    )
    return pl.pallas_call(
        _moe_kernel,
        grid_spec=grid_spec,
        out_shape=jax.ShapeDtypeStruct((Y_SLABS, n_rows, d // Y_SLABS), F32),
        compiler_params=_cparams(("arbitrary",)),
        name="moe_experts",
    )(blk_e, n_used, blk_valid, xg, wgu, bgu, wd, bd)


def _combine_kernel(x_ref, y_ref, rw_ref, gt2_ref, gfin_ref, o_ref, *, final):
    rw = rw_ref[...]
    s = None
    for j in range(TOP_K):
        y_j = jnp.concatenate([y_ref[c, j] for c in range(y_ref.shape[0])], axis=1) * rw[:, j:j + 1]
        s = y_j if s is None else s + y_j
    x2 = x_ref[...] + gt2_ref[0] * s
    if final:
        x2 = _rms(x2) * gfin_ref[...]
    o_ref[...] = x2


def _combine_call(rows, n_tiles, x1, yg, rw, mod3, gfin, final):
    d = x1.shape[1]
    return pl.pallas_call(
        functools.partial(_combine_kernel, final=final),
        grid=(n_tiles,),
        in_specs=[pl.BlockSpec((TM, d), lambda i: (i, 0)),
                  pl.BlockSpec((Y_SLABS, TOP_K, TM, d // Y_SLABS), lambda i: (0, 0, i, 0)),
                  pl.BlockSpec((TM, LANES), lambda i: (i, 0)),
                  rows.mod_spec(5, d),
                  pl.BlockSpec((1, d), lambda i: (0, 0))],
        out_specs=pl.BlockSpec((TM, d), lambda i: (i, 0)),
        out_shape=jax.ShapeDtypeStruct((n_tiles * TM, d), F32),
        compiler_params=_cparams(("parallel",)),
        name="moe_combine_final" if final else "moe_combine",
    )(x1, yg, rw, mod3, gfin)


def _sc_mesh():
    return plsc.VectorSubcoreMesh(core_axis_name="core", subcore_axis_name="subcore")


def _sc_scatter_rows(src, idx_list, n_out):
    r, c = src.shape
    assert r % SC_WINDOW == 0

    @pl.kernel(out_type=jax.ShapeDtypeStruct((n_out, c), src.dtype), mesh=_sc_mesh(), scratch_types=[])
    def scatter(x_hbm, *refs):
        idx_hbm, o_hbm = refs[:-1], refs[-1]

        def body(x_vmem, *idx_vmem):
            for iv in idx_vmem:
                pltpu.sync_copy(x_vmem, o_hbm.at[iv.at[0]])

        ispec = pl.BlockSpec((1, SC_WINDOW), lambda i: (0, i))
        pltpu.emit_pipeline(
            body, grid=(r // SC_WINDOW,),
            in_specs=[pl.BlockSpec((SC_WINDOW, c), lambda i: (i, 0))] + [ispec] * len(idx_hbm),
            out_specs=[], core_axis_name=("core", "subcore"),
            dimension_semantics=(pltpu.PARALLEL,))(x_hbm, *idx_hbm)

    return scatter(src, *idx_list)


def _sc_gather_rows(src, idx):
    r, c = idx.shape[1], src.shape[1]
    assert r % SC_WINDOW == 0

    @pl.kernel(out_type=jax.ShapeDtypeStruct((r, c), src.dtype), mesh=_sc_mesh(), scratch_types=[])
    def gather(x_hbm, i_hbm, o_hbm):
        def body(i_vmem, o_vmem):
            pltpu.sync_copy(x_hbm.at[i_vmem.at[0]], o_vmem)

        pltpu.emit_pipeline(
            body, grid=(r // SC_WINDOW,),
            in_specs=[pl.BlockSpec((1, SC_WINDOW), lambda i: (0, i))],
            out_specs=[pl.BlockSpec((SC_WINDOW, c), lambda i: (i, 0))],
            core_axis_name=("core", "subcore"),
            dimension_semantics=(pltpu.PARALLEL,))(i_hbm, o_hbm)

    return gather(src, idx)


def _route(top_idx, rank, counts):
    t = top_idx.shape[0]
    m = t * TOP_K
    padded = ((counts + MOE_TB - 1) // MOE_TB) * MOE_TB
    pend = jnp.cumsum(padded)
    pstart = pend - padded
    onehot = top_idx[:, :, None] == jnp.arange(N_EXPERTS, dtype=jnp.int32)[None, None, :]
    dest = jnp.sum(jnp.where(onehot, pstart[None, None, :], 0), axis=-1) + rank
    n_blk = -(-m // MOE_TB) + N_EXPERTS
    n_used = (pend[-1] // MOE_TB).astype(jnp.int32)
    blk = jnp.arange(n_blk, dtype=jnp.int32)
    blk_e = jnp.sum((blk[:, None] * MOE_TB >= pend[None, :]).astype(jnp.int32), axis=1)
    blk_e = jnp.minimum(blk_e, N_EXPERTS - 1)
    e_hot = blk_e[:, None] == jnp.arange(N_EXPERTS, dtype=jnp.int32)[None, :]
    first = jnp.sum(jnp.where(e_hot, pstart[None, :], 0), axis=1) // MOE_TB
    cnt_e = jnp.sum(jnp.where(e_hot, counts[None, :], 0), axis=1)
    blk_valid = jnp.clip(cnt_e - (blk - first) * MOE_TB, 0, MOE_TB).astype(jnp.int32)
    last_e = jnp.sum(jnp.where(blk == n_used - 1, blk_e, 0))
    blk_e = jnp.where(blk < n_used, blk_e, last_e).astype(jnp.int32)
    return dest, blk_e, n_used.reshape(1), blk_valid, n_blk * MOE_TB


def _rope_table(n, rot_dim, lane_off, period):
    qd = rot_dim // 4
    pos = jnp.arange(n, dtype=jnp.int32)
    rws = (pos // GRID_W).astype(F32)
    cls = (pos % GRID_W).astype(F32)
    inv = ROPE_BASE ** (-jnp.arange(qd, dtype=F32) / qd)
    lane = jnp.arange(LANES, dtype=jnp.int32)
    w = (lane - lane_off) % period
    active = (lane >= lane_off) & (w < rot_dim) if period == LANES else jnp.ones((LANES,), bool)
    slot = w // qd
    freq = inv[w % qd]
    ang = jnp.where((slot < 2)[None, :], rws[:, None], cls[:, None]) * freq[None, :]
    sign = jnp.where(slot % 2 == 0, -1.0, 1.0)
    cos = jnp.where(active[None, :], jnp.cos(ang), 1.0)
    sin = jnp.where(active[None, :], jnp.sin(ang) * sign[None, :], 0.0)
    ident = (jnp.ones((TM, LANES), F32), jnp.zeros((TM, LANES), F32))
    return jnp.concatenate([cos, ident[0]], 0), jnp.concatenate([sin, ident[1]], 0)


def _prep_weights(p):
    w_in = p["w_in"]
    nl, d, _ = w_in.shape
    o = {}
    o["w_a"] = w_in[:, :, :3 * A_WIDTH].astype(BF16)
    o["w_b"] = w_in[:, :, 3 * A_WIDTH:3 * A_WIDTH + 3 * B_WIDTH].astype(BF16)
    c0 = 3 * A_WIDTH + 3 * B_WIDTH
    c1 = c0 + C_Q_RANK + C_KV_RANK
    zeros = lambda k: jnp.zeros((nl, d, k), F32)
    o["w_c"] = jnp.concatenate([w_in[:, :, c0:c1], zeros(C_NOPE), w_in[:, :, c1:c1 + C_ROPE],
                                zeros(LANES - C_NOPE - C_ROPE)], axis=-1).astype(BF16)
    o["w_g"] = w_in[:, :, c1 + C_ROPE:].astype(BF16)
    wq = p["w_q_b"].reshape(nl, C_Q_RANK, C_HEADS, C_NOPE + C_ROPE)
    wq = jnp.pad(wq, ((0, 0), (0, 0), (0, 0), (0, LANES - C_NOPE - C_ROPE)))
    o["w_q_b"] = wq.reshape(nl, C_Q_RANK, C_HEADS * LANES).astype(BF16)
    wkv = p["w_kv_b"].reshape(nl, C_KV_RANK, C_HEADS, C_NOPE + C_VDIM)
    wk = jnp.pad(wkv[..., :C_NOPE], ((0, 0), (0, 0), (0, 0), (0, LANES - C_NOPE)))
    o["w_kv_b"] = jnp.concatenate([wk.reshape(nl, C_KV_RANK, C_HEADS * LANES),
                                   wkv[..., C_NOPE:].reshape(nl, C_KV_RANK, C_WIDTH)], axis=-1).astype(BF16)
    for name in ("w_br_a", "w_br_b", "w_br_c", "w_out"):
        o[name] = p[name].astype(BF16)
    o["w_gate_up"] = p["w_gate_up"].reshape((-1,) + p["w_gate_up"].shape[2:])
    o["w_down"] = p["w_down"].reshape((-1,) + p["w_down"].shape[2:])
    o["b_gate_up"] = _regroup_gate_up(p["b_gate_up"]).reshape(-1, 1, p["b_gate_up"].shape[-1])
    o["b_down"] = p["b_down"].reshape(-1, 1, p["b_down"].shape[-1])
    o["w_router"] = jnp.pad(p["w_router"], ((0, 0), (0, 0), (0, LANES - N_EXPERTS)))
    o["b_router"] = jnp.pad(p["b_router"], ((0, 0), (0, LANES - N_EXPERTS)))[:, None, :]
    return o


@jax.jit
def _trunk(p):
    x, ctx = p["x"], p["ctx"]
    bsz, n, d = x.shape
    n_ctx = ctx.shape[1]
    depth = p["w_mod"].shape[0]
    rows = _Rows(bsz, n, n_ctx)
    w = _prep_weights(p)
    rope_a = _rope_table(n, A_HEAD_DIM, 0, A_HEAD_DIM)
    rope_c = _rope_table(n, C_ROPE, C_NOPE, LANES)
    scale_a = A_HEAD_DIM ** -0.5 * LOG2E
    scale_b = B_HEAD_DIM ** -0.5 * LOG2E
    scale_c = (C_NOPE + C_ROPE) ** -0.5 * LOG2E
    c8 = jnp.zeros((8, d), F32).at[:bsz].set(p["c"]).at[bsz].set(p["c_ctx"])
    x_all = jnp.concatenate([x.reshape(bsz * n, d), ctx.reshape(bsz * n_ctx, d)], axis=0)
    dummy_lam = jnp.zeros((4, A_HEAD_DIM), F32)
    dummy_g = jnp.ones((1, LANES), F32)
    out = None
    for l in range(depth):
        last = l == depth - 1
        lam_init = 0.8 - 0.6 * math.exp(-0.3 * l)
        mod3 = _mod_call(c8, p["w_mod"][l], p["b_mod"][l]).reshape(8 * N_MOD, 1, d)
        g_mix = p["g_mix"][l][None, :]
        qa, ka, va = _inproj_qkv_call(rows, x_all, g_mix, mod3, w["w_a"][l], rope_a, scale_a)
        qb, kb, vb = _inproj_qkv_call(rows, x_all, g_mix, mod3, w["w_b"][l], None, scale_b)
        qc, kc, vc = _inproj_mla_call(rows, x_all, g_mix, mod3, w["w_c"][l], p["g_q_a"][l][None, :],
                                      w["w_q_b"][l], p["g_kv_a"][l][None, :], w["w_kv_b"][l], rope_c, scale_c)
        gates = _inproj_gates_call(rows, x_all, g_mix, mod3, w["w_g"][l])
        lamv = jnp.stack([p["lam_q1"][l], p["lam_k1"][l], p["lam_q2"][l], p["lam_k2"][l]])
        gs = p["g_subln"][l][None, :]
        attn = functools.partial(_pair_attention_call, rows)
        ya = attn(qa, ka, va, lamv, gs, wq=LANES, mode="diff", lam_init=lam_init, ctx_queries=False)
        yb = _nbr_attention_call(rows, qb, kb, vb, p["rpb"][l])
        yc = attn(qc, kc, vc, dummy_lam, dummy_g, wq=2 * LANES, mode="pair", lam_init=0.0, ctx_queries=False)
        if not last:
            ya_c = attn(qa, ka, va, lamv, gs, wq=LANES, mode="diff", lam_init=lam_init, ctx_queries=True)
            yb_c = attn(qb, kb, vb, dummy_lam, dummy_g, wq=LANES, mode="pair", lam_init=0.0, ctx_queries=True)
            yc_c = attn(qc, kc, vc, dummy_lam, dummy_g, wq=2 * LANES, mode="pair", lam_init=0.0, ctx_queries=True)
            ya, yb, yc = (jnp.concatenate(pair, axis=0) for pair in ((ya, ya_c), (yb, yb_c), (yc, yc_c)))
        n_tiles = rows.lat_tiles if last else rows.all_tiles
        x1, h2, ri, rw, rk, cnt = _merge_call(rows, n_tiles, ya, yb, yc, gates, x_all, w["w_br_a"][l], w["w_br_b"][l],
                                     w["w_br_c"][l], w["w_out"][l], mod3, p["g_ffn"][l][None, :],
                                     w["w_router"][l], w["b_router"][l])
        dest, blk_e, n_used, blk_valid, n_rows = _route(ri[:, :TOP_K], rk[:, :TOP_K],
                                                        cnt[0, :N_EXPERTS].astype(jnp.int32))
        t_rows = n_tiles * TM
        n_pieces = h2.shape[0]
        piece_off = (jnp.arange(n_pieces, dtype=jnp.int32) * n_rows)[:, None]
        xg = _sc_scatter_rows(h2.reshape(n_pieces * t_rows, LANES),
                              [(piece_off + dest[None, :, j]).reshape(1, -1) for j in range(TOP_K)],
                              n_pieces * n_rows).reshape(n_pieces, n_rows, LANES)
        y = _moe_call(xg, blk_e + l * N_EXPERTS, n_used, blk_valid, w["w_gate_up"], w["b_gate_up"],
                      w["w_down"], w["b_down"])
        slab_off = (jnp.arange(Y_SLABS, dtype=jnp.int32) * n_rows)[:, None, None]
        yg = _sc_gather_rows(y.reshape(Y_SLABS * n_rows, d // Y_SLABS),
                             (slab_off + dest.T[None]).reshape(1, -1)).reshape(Y_SLABS, TOP_K, t_rows, d // Y_SLABS)
        x_new = _combine_call(rows, n_tiles, x1, yg, rw, mod3, p["g_final"][None, :], last)
        if last:
            out = x_new.reshape(bsz, n, d)
        else:
            x_all = x_new
    return out


def kernel(x, c, ctx, c_ctx, w_mod, b_mod, g_mix, w_in, lam_q1, lam_k1, lam_q2, lam_k2, g_subln, rpb, g_q_a, w_q_b, g_kv_a, w_kv_b, w_br_a, w_br_b, w_br_c, w_out, g_ffn, w_router, b_router, w_gate_up, b_gate_up, w_down, b_down, g_final):
    return _trunk(dict(x=x, c=c, ctx=ctx, c_ctx=c_ctx, w_mod=w_mod, b_mod=b_mod, g_mix=g_mix, w_in=w_in,
                       lam_q1=lam_q1, lam_k1=lam_k1, lam_q2=lam_q2, lam_k2=lam_k2, g_subln=g_subln, rpb=rpb,
                       g_q_a=g_q_a, w_q_b=w_q_b, g_kv_a=g_kv_a, w_kv_b=w_kv_b, w_br_a=w_br_a, w_br_b=w_br_b,
                       w_br_c=w_br_c, w_out=w_out, g_ffn=g_ffn, w_router=w_router, b_router=b_router,
                       w_gate_up=w_gate_up, b_gate_up=b_gate_up, w_down=w_down, b_down=b_down, g_final=g_final))
```

```python
import functools
import math

import jax
import jax.numpy as jnp
from jax import lax
from jax.experimental import pallas as pl
from jax.experimental.pallas import tpu as pltpu
from jax.experimental.pallas import tpu_sc as plsc

GRID_W = 64
A_HEADS = 4
A_HEAD_DIM = 64
A_WIDTH = A_HEADS * 2 * A_HEAD_DIM
B_HEADS = 8
B_HEAD_DIM = 64
B_WIDTH = B_HEADS * B_HEAD_DIM
NA_ROWS = 8
NA_COLS = 16
C_HEADS = 8
C_NOPE = 64
C_ROPE = 32
C_VDIM = 64
C_Q_RANK = 384
C_KV_RANK = 256
C_WIDTH = C_HEADS * C_VDIM
N_BRANCH = 3
N_EXPERTS = 32
TOP_K = 4
SWIGLU_LIMIT = 7.0
SWIGLU_ALPHA = 1.702
N_MOD = 6
ROPE_BASE = 10000.0
EPS = 1e-6
NEG_INF = -1e30
LOG2E = 1.4426950408889634

LANES = 128
VMEM_LIMIT = 56 * 1024 * 1024
TM = 512
TQ = 512
TK = 2048
MOE_TB = 512
Y_SLABS = 4
SC_WINDOW = 128
COMBINE_PARTS = 2
NB_GROUP = 4
NB_GROUPS_PER_STEP = 16

BF16 = jnp.bfloat16
F32 = jnp.float32


def _cparams(sem):
    return pltpu.CompilerParams(dimension_semantics=sem, vmem_limit_bytes=VMEM_LIMIT)


def _dot(a, b):
    return jnp.dot(a, b, preferred_element_type=F32)


def _dot_nt(a, b):
    return lax.dot_general(a, b, (((1,), (1,)), ((), ())), preferred_element_type=F32)


def _split_bf16(a):
    hi = a.astype(BF16)
    lo = (a - hi.astype(F32)).astype(BF16)
    return hi, lo


def _dot_split(a, b):
    a_hi, a_lo = _split_bf16(a)
    b_hi, b_lo = _split_bf16(b)
    return _dot(a_hi, b_hi) + _dot(a_hi, b_lo) + _dot(a_lo, b_hi)


def _rms(x):
    return x * lax.rsqrt(jnp.mean(x * x, axis=-1, keepdims=True) + EPS)


def _modulated(x_ref, g_ref, sh_ref, sc_ref):
    return (_rms(x_ref[...]) * g_ref[...]) * (1.0 + sc_ref[0]) + sh_ref[0]


def _rope(blk, cos, sin_signed, shift):
    lane = lax.broadcasted_iota(jnp.int32, blk.shape, 1)
    first = ((lane // shift) % 2) == 0
    partner = jnp.where(first, pltpu.roll(blk, LANES - shift, 1), pltpu.roll(blk, shift, 1))
    return blk * cos + partner * sin_signed


def _mod_kernel(c_ref, w_ref, b_ref, o_ref):
    cc = c_ref[...]
    a = cc * jax.nn.sigmoid(cc)
    o_ref[...] = _dot_split(a, w_ref[...]) + b_ref[...]


def _mod_call(c8, w_mod, b_mod):
    d = c8.shape[1]
    n_out = w_mod.shape[1]
    return pl.pallas_call(
        _mod_kernel,
        grid=(n_out // d,),
        in_specs=[pl.BlockSpec((8, d), lambda j: (0, 0)),
                  pl.BlockSpec((d, d), lambda j: (0, j)),
                  pl.BlockSpec((1, d), lambda j: (0, j))],
        out_specs=pl.BlockSpec((8, d), lambda j: (0, j)),
        out_shape=jax.ShapeDtypeStruct((8, n_out), F32),
        compiler_params=_cparams(("arbitrary",)),
        name="mod",
    )(c8, w_mod, b_mod.reshape(1, n_out))


class _Rows:
    def __init__(self, bsz, n, n_ctx):
        assert n % TM == 0 and (bsz * n_ctx) % TM == 0
        self.bsz, self.n, self.n_ctx = bsz, n, n_ctx
        self.t_lat = bsz * n
        self.t_all = bsz * n + bsz * n_ctx
        self.tiles_per_batch = n // TM
        self.lat_tiles = self.t_lat // TM
        self.all_tiles = self.t_all // TM

    def mod_spec(self, which, d, first_tile=0):
        tpb, bsz = self.tiles_per_batch, self.bsz
        return pl.BlockSpec((1, 1, d), lambda i, *_: (jnp.minimum((i + first_tile) // tpb, bsz) * N_MOD + which, 0, 0))

    def rope_spec(self):
        tpb, lat = self.tiles_per_batch, self.lat_tiles
        return pl.BlockSpec((TM, LANES), lambda i, *_: (jnp.where(i < lat, i % tpb, tpb), 0))


def _inproj_qkv_kernel(*refs, rope, qscale):
    if rope:
        x_ref, g_ref, sh_ref, sc_ref, w_ref, cos_ref, sin_ref, q_ref, k_ref, v_ref = refs
    else:
        x_ref, g_ref, sh_ref, sc_ref, w_ref, q_ref, k_ref, v_ref = refs
    h = _modulated(x_ref, g_ref, sh_ref, sc_ref).astype(BF16)
    t = _dot(h, w_ref[...])
    outs = (q_ref, k_ref, v_ref)
    per = q_ref.shape[1] // LANES
    for j in range(3 * per):
        blk = t[:, j * LANES:(j + 1) * LANES]
        if rope and j < 2 * per:
            blk = _rope(blk, cos_ref[...], sin_ref[...], A_HEAD_DIM // 4)
        if j < per:
            blk = blk * qscale
        outs[j // per][:, (j % per) * LANES:(j % per + 1) * LANES] = blk.astype(BF16)


def _inproj_qkv_call(rows, x_all, g, mod3, w, tables, qscale):
    d = x_all.shape[1]
    width = w.shape[1] // 3
    rope = tables is not None
    in_specs = [pl.BlockSpec((TM, d), lambda i: (i, 0)),
                pl.BlockSpec((1, d), lambda i: (0, 0)),
                rows.mod_spec(0, d), rows.mod_spec(1, d),
                pl.BlockSpec((d, 3 * width), lambda i: (0, 0))]
    args = [x_all, g, mod3, mod3, w]
    if rope:
        in_specs += [rows.rope_spec(), rows.rope_spec()]
        args += list(tables)
    out = jax.ShapeDtypeStruct((rows.t_all, width), BF16)
    return pl.pallas_call(
        functools.partial(_inproj_qkv_kernel, rope=rope, qscale=qscale),
        grid=(rows.all_tiles,),
        in_specs=in_specs,
        out_specs=[pl.BlockSpec((TM, width), lambda i: (i, 0))] * 3,
        out_shape=[out] * 3,
        compiler_params=_cparams(("parallel",)),
        name="inproj_qkv_rope" if rope else "inproj_qkv",
    )(*args)


def _inproj_gates_kernel(x_ref, g_ref, sh_ref, sc_ref, w_ref, o_ref):
    h = _modulated(x_ref, g_ref, sh_ref, sc_ref).astype(BF16)
    d = x_ref.shape[1]
    for j in range(o_ref.shape[1] // d):
        o_ref[:, j * d:(j + 1) * d] = jax.nn.sigmoid(_dot(h, w_ref[:, j * d:(j + 1) * d])).astype(BF16)


def _inproj_gates_call(rows, x_all, g, mod3, w):
    d = x_all.shape[1]
    n_out = w.shape[1]
    return pl.pallas_call(
        _inproj_gates_kernel,
        grid=(rows.all_tiles,),
        in_specs=[pl.BlockSpec((TM, d), lambda i: (i, 0)),
                  pl.BlockSpec((1, d), lambda i: (0, 0)),
                  rows.mod_spec(0, d), rows.mod_spec(1, d),
                  pl.BlockSpec((d, n_out), lambda i: (0, 0))],
        out_specs=pl.BlockSpec((TM, n_out), lambda i: (i, 0)),
        out_shape=jax.ShapeDtypeStruct((rows.t_all, n_out), BF16),
        compiler_params=_cparams(("parallel",)),
        name="inproj_gates",
    )(x_all, g, mod3, mod3, w)


def _inproj_mla_kernel(x_ref, g_ref, sh_ref, sc_ref, w_ref, gq_ref, wq_ref, gkv_ref, wkv_ref,
                       cos_ref, sin_ref, q_ref, k_ref, v_ref, *, qscale):
    h = _modulated(x_ref, g_ref, sh_ref, sc_ref).astype(BF16)
    t = _dot(h, w_ref[...])
    cos, sin = cos_ref[...], sin_ref[...]
    shift = C_ROPE // 4
    cq = (_rms(t[:, :C_Q_RANK]) * gq_ref[...]).astype(BF16)
    q = _dot(cq, wq_ref[...])
    for j in range(C_HEADS):
        blk = _rope(q[:, j * LANES:(j + 1) * LANES], cos, sin, shift) * qscale
        q_ref[:, j * LANES:(j + 1) * LANES] = blk.astype(BF16)
    ckv = (_rms(t[:, C_Q_RANK:C_Q_RANK + C_KV_RANK]) * gkv_ref[...]).astype(BF16)
    kv = _dot(ckv, wkv_ref[...])
    pe = _rope(t[:, C_Q_RANK + C_KV_RANK:], cos, sin, shift)
    for j in range(C_HEADS):
        k_ref[:, j * LANES:(j + 1) * LANES] = (kv[:, j * LANES:(j + 1) * LANES] + pe).astype(BF16)
    v_ref[...] = kv[:, C_HEADS * LANES:].astype(BF16)


def _inproj_mla_call(rows, x_all, g, mod3, w, gq, wq, gkv, wkv, tables, qscale):
    d = x_all.shape[1]
    full = lambda a: pl.BlockSpec(a.shape, lambda i: (0,) * a.ndim)
    qk = jax.ShapeDtypeStruct((rows.t_all, C_HEADS * LANES), BF16)
    return pl.pallas_call(
        functools.partial(_inproj_mla_kernel, qscale=qscale),
        grid=(rows.all_tiles,),
        in_specs=[pl.BlockSpec((TM, d), lambda i: (i, 0)), full(g),
                  rows.mod_spec(0, d), rows.mod_spec(1, d),
                  full(w), full(gq), full(wq), full(gkv), full(wkv),
                  rows.rope_spec(), rows.rope_spec()],
        out_specs=[pl.BlockSpec((TM, C_HEADS * LANES), lambda i: (i, 0)),
                   pl.BlockSpec((TM, C_HEADS * LANES), lambda i: (i, 0)),
                   pl.BlockSpec((TM, C_WIDTH), lambda i: (i, 0))],
        out_shape=[qk, qk, jax.ShapeDtypeStruct((rows.t_all, C_WIDTH), BF16)],
        compiler_params=_cparams(("parallel",)),
        name="inproj_mla",
    )(x_all, g, mod3, mod3, w, gq, wq, gkv, wkv, *tables)


def _pair_attention_kernel(*refs, n_lat_chunks, tk, mode, lam_init):
    if n_lat_chunks:
        q_ref, kl_ref, vl_ref, kc_ref, vc_ref, lam_ref, gs_ref, o_ref = refs
    else:
        q_ref, kc_ref, vc_ref, lam_ref, gs_ref, o_ref = refs
    q = q_ref[...]
    tq, wq = q.shape
    lane = lax.broadcasted_iota(jnp.int32, q.shape, 1)
    lower = lane < wq // 2
    zero = jnp.zeros_like(q)
    qs = jnp.concatenate([jnp.where(lower, q, zero), jnp.where(lower, zero, q)], axis=0)

    def step(kc, vc, carry):
        m, l, acc = carry
        s = _dot_nt(qs, kc)
        m_new = jnp.maximum(m, jnp.max(s, axis=-1, keepdims=True))
        p = jnp.exp2(s - m_new)
        alpha = jnp.exp2(m - m_new)
        l = alpha * l + jnp.sum(p, axis=-1, keepdims=True)
        acc = alpha * acc + _dot(p.astype(BF16), vc)
        return m_new, l, acc

    carry = (jnp.full((2 * tq, 1), NEG_INF, F32), jnp.zeros((2 * tq, 1), F32),
             jnp.zeros((2 * tq, LANES), F32))
    for i in range(n_lat_chunks):
        carry = step(kl_ref[i * tk:(i + 1) * tk, :], vl_ref[i * tk:(i + 1) * tk, :], carry)
    m, l, acc = step(kc_ref[...], vc_ref[...], carry)
    o = acc / l
    o0, o1 = o[:tq], o[tq:]
    if mode == "diff":
        lv = lam_ref[...]
        lam = (jnp.exp(jnp.sum(lv[0:1] * lv[1:2], axis=-1, keepdims=True))
               - jnp.exp(jnp.sum(lv[2:3] * lv[3:4], axis=-1, keepdims=True)) + lam_init)
        out = (_rms(o0 - lam * o1) * gs_ref[...]) * (1.0 - lam_init)
    else:
        out = jnp.where(lax.broadcasted_iota(jnp.int32, o0.shape, 1) < LANES // 2, o0, o1)
    o_ref[...] = out.astype(BF16)


def _pair_attention_call(rows, q, k, v, lamv, gs, *, wq, mode, lam_init, ctx_queries):
    bsz, n, n_ctx = rows.bsz, rows.n, rows.n_ctx
    n_pairs = v.shape[1] // LANES
    ctx_blk0 = rows.t_lat // n_ctx
    kc_spec = pl.BlockSpec((n_ctx, wq), lambda b, h, i: (ctx_blk0 + b, h))
    vc_spec = pl.BlockSpec((n_ctx, LANES), lambda b, h, i: (ctx_blk0 + b, h))
    small = [pl.BlockSpec(lamv.shape, lambda b, h, i: (0, 0)), pl.BlockSpec(gs.shape, lambda b, h, i: (0, 0))]
    if ctx_queries:
        tq, n_q, n_chunks, tk = n_ctx, 1, 0, 0
        q_spec = pl.BlockSpec((tq, wq), lambda b, h, i: (ctx_blk0 + b, h))
        o_spec = pl.BlockSpec((tq, LANES), lambda b, h, i: (b, h))
        in_specs = [q_spec, kc_spec, vc_spec] + small
        args = (q, k, v, lamv, gs)
        out_rows = bsz * n_ctx
    else:
        tk = min(TK, n)
        tq = min(TQ, n)
        assert n % tq == 0 and n % tk == 0
        n_q, n_chunks = n // tq, n // tk
        q_spec = pl.BlockSpec((tq, wq), lambda b, h, i: (b * n_q + i, h))
        o_spec = pl.BlockSpec((tq, LANES), lambda b, h, i: (b * n_q + i, h))
        in_specs = [q_spec,
                    pl.BlockSpec((n, wq), lambda b, h, i: (b, h)),
                    pl.BlockSpec((n, LANES), lambda b, h, i: (b, h)),
                    kc_spec, vc_spec] + small
        args = (q, k, v, k, v, lamv, gs)
        out_rows = rows.t_lat
    return pl.pallas_call(
        functools.partial(_pair_attention_kernel, n_lat_chunks=n_chunks, tk=tk, mode=mode, lam_init=lam_init),
        grid=(bsz, n_pairs, n_q),
        in_specs=in_specs,
        out_specs=o_spec,
        out_shape=jax.ShapeDtypeStruct((out_rows, n_pairs * LANES), BF16),
        compiler_params=_cparams(("parallel", "parallel", "arbitrary")),
        name=f"attn_{mode}_{'ctx' if ctx_queries else 'lat'}_{wq}",
    )(*args)


def _nbr_groups(n_rows):
    half = NA_ROWS // 2
    span = NA_ROWS + NB_GROUP - 1
    starts, classes, patterns = [], [], []
    for r0 in range(0, n_rows, NB_GROUP):
        rs = [min(max(r - half, 0), n_rows - NA_ROWS) for r in range(r0, r0 + NB_GROUP)]
        us = min(rs[0], n_rows - span)
        pat = (tuple(x - us for x in rs), tuple(r - us for r in range(r0, r0 + NB_GROUP)))
        if pat not in patterns:
            patterns.append(pat)
        starts.append(us)
        classes.append(patterns.index(pat))
    return starts, classes, patterns


def _nbr_bias_table(rpb, patterns):
    span = NA_ROWS + NB_GROUP - 1
    c_idx = jnp.arange(GRID_W, dtype=jnp.int32)
    col_start = jnp.clip(c_idx - NA_COLS // 2, 0, GRID_W - NA_COLS)
    col_mask = (c_idx[None, :] >= col_start[:, None]) & (c_idx[None, :] < col_start[:, None] + NA_COLS)
    col_bias_idx = jnp.clip(c_idx[None, :] - c_idx[:, None] + NA_COLS - 1, 0, 2 * NA_COLS - 2)
    rpb_c = rpb[:, :, col_bias_idx] * LOG2E
    u = jnp.arange(span)
    tabs = []
    for rs_rel, r_rel in patterns:
        per_row = []
        for a in range(NB_GROUP):
            valid = (u >= rs_rel[a]) & (u < rs_rel[a] + NA_ROWS)
            di = jnp.clip(u - r_rel[a] + NA_ROWS - 1, 0, 2 * NA_ROWS - 2)
            bias = jnp.transpose(rpb_c[:, di], (0, 2, 1, 3))
            ok = valid[None, None, :, None] & col_mask[None, :, None, :]
            per_row.append(jnp.where(ok, bias, NEG_INF).reshape(rpb.shape[0], GRID_W, span * GRID_W))
        tabs.append(jnp.concatenate(per_row, axis=1))
    return jnp.stack(tabs).astype(F32)


def _nbr_attention_kernel(us_ref, cls_ref, q_ref, kl_ref, vl_ref, kc_ref, vc_ref, bias_ref, o_ref, *, groups_per_step):
    kc, vc = kc_ref[...], vc_ref[...]
    rows_q = NB_GROUP * GRID_W
    win = (NA_ROWS + NB_GROUP - 1) * GRID_W

    def one_group(j, _):
        g = pl.program_id(2) * groups_per_step + j
        q = q_ref[pl.ds(pl.multiple_of(j * rows_q, rows_q), rows_q), :]
        lane = lax.broadcasted_iota(jnp.int32, q.shape, 1)
        lower = lane < LANES // 2
        zero = jnp.zeros_like(q)
        qs = jnp.concatenate([jnp.where(lower, q, zero), jnp.where(lower, zero, q)], axis=0)
        off = pl.multiple_of(us_ref[g] * GRID_W, GRID_W)
        kw = kl_ref[pl.ds(off, win), :]
        vw = vl_ref[pl.ds(off, win), :]
        bias = bias_ref[cls_ref[g]].reshape(2 * rows_q, win)
        s = jnp.where(bias > 0.5 * NEG_INF, _dot_nt(qs, kw) + bias, NEG_INF)
        s_ctx = _dot_nt(qs, kc)
        m = jnp.maximum(jnp.max(s, axis=-1, keepdims=True), jnp.max(s_ctx, axis=-1, keepdims=True))
        p = jnp.exp2(s - m)
        p_ctx = jnp.exp2(s_ctx - m)
        l = jnp.sum(p, axis=-1, keepdims=True) + jnp.sum(p_ctx, axis=-1, keepdims=True)
        o = (_dot(p.astype(BF16), vw) + _dot(p_ctx.astype(BF16), vc)) / l
        out = jnp.where(lax.broadcasted_iota(jnp.int32, (rows_q, LANES), 1) < LANES // 2,
                        o[:rows_q], o[rows_q:])
        o_ref[pl.ds(pl.multiple_of(j * rows_q, rows_q), rows_q), :] = out.astype(BF16)
        return 0

    lax.fori_loop(0, groups_per_step, one_group, 0, unroll=True)


def _nbr_attention_call(rows, q, k, v, rpb):
    bsz, n, n_ctx = rows.bsz, rows.n, rows.n_ctx
    n_rows = n // GRID_W
    groups_per_step = min(NB_GROUPS_PER_STEP, n_rows // NB_GROUP)
    rows_per_step = NB_GROUP * groups_per_step
    assert n_rows % rows_per_step == 0 and n_rows >= NA_ROWS + NB_GROUP - 1
    starts, classes, patterns = _nbr_groups(n_rows)
    bias = _nbr_bias_table(rpb, patterns)
    steps = n_rows // rows_per_step
    n_pairs = B_HEADS // 2
    ctx_blk0 = rows.t_lat // n_ctx
    tq = rows_per_step * GRID_W
    grid_spec = pltpu.PrefetchScalarGridSpec(
        num_scalar_prefetch=2,
        grid=(bsz, n_pairs, steps),
        in_specs=[pl.BlockSpec((tq, LANES), lambda b, h, i, us, cl: (b * steps + i, h)),
                  pl.BlockSpec((n, LANES), lambda b, h, i, us, cl: (b, h)),
                  pl.BlockSpec((n, LANES), lambda b, h, i, us, cl: (b, h)),
                  pl.BlockSpec((n_ctx, LANES), lambda b, h, i, us, cl: (ctx_blk0 + b, h)),
                  pl.BlockSpec((n_ctx, LANES), lambda b, h, i, us, cl: (ctx_blk0 + b, h)),
                  pl.BlockSpec((bias.shape[0], 2) + bias.shape[2:], lambda b, h, i, us, cl: (0, h, 0, 0))],
        out_specs=pl.BlockSpec((tq, LANES), lambda b, h, i, us, cl: (b * steps + i, h)),
    )
    return pl.pallas_call(
        functools.partial(_nbr_attention_kernel, groups_per_step=groups_per_step),
        grid_spec=grid_spec,
        out_shape=jax.ShapeDtypeStruct((rows.t_lat, B_WIDTH), BF16),
        compiler_params=_cparams(("parallel", "parallel", "arbitrary")),
        name="attn_nbr",
    )(jnp.asarray(starts, jnp.int32), jnp.asarray(classes, jnp.int32), q, k, v, k, v, bias)


def _merge_kernel(ya_ref, yb_ref, yc_ref, g_ref, x_ref, wa_ref, wb_ref, wc_ref, wo_ref,
                  gt1_ref, gffn_ref, sh2_ref, sc2_ref, wr_ref, br_ref,
                  x1_ref, h2_ref, ri_ref, rw_ref, rk_ref, cnt_ref):
    d = x_ref.shape[1]

    @pl.when(pl.program_id(0) == 0)
    def _():
        cnt_ref[...] = jnp.zeros_like(cnt_ref)

    g = g_ref[...].astype(F32)
    m = (g[:, :d] * _dot(ya_ref[...], wa_ref[...])
         + g[:, d:2 * d] * _dot(yb_ref[...], wb_ref[...])
         + g[:, 2 * d:] * _dot(yc_ref[...], wc_ref[...]))
    y = _dot(m.astype(BF16), wo_ref[...])
    x1 = x_ref[...] + gt1_ref[0] * y
    x1_ref[...] = x1
    h2 = (_rms(x1) * gffn_ref[...]) * (1.0 + sc2_ref[0]) + sh2_ref[0]
    hb = lax.bitcast_convert_type(h2.astype(BF16).astype(F32), jnp.uint32)
    words = hb[:, :d // 2] | (hb[:, d // 2:] >> 16)
    for c in range(h2_ref.shape[0]):
        h2_ref[c] = words[:, c * LANES:(c + 1) * LANES]
    logits = _dot_split(h2, wr_ref[...]) + br_ref[...]
    lane = lax.broadcasted_iota(jnp.int32, logits.shape, 1)
    lane_f = lane.astype(F32)
    work = jnp.where(lane < N_EXPERTS, logits, NEG_INF)
    idx_out = jnp.zeros(logits.shape, F32)
    val_out = jnp.zeros(logits.shape, F32)
    onehot = jnp.zeros(logits.shape, F32)
    top = None
    den = jnp.zeros((logits.shape[0], 1), F32)
    idxs = []
    for j in range(TOP_K):
        mx = jnp.max(work, axis=-1, keepdims=True)
        idx = jnp.min(jnp.where(work == mx, lane_f, float(LANES)), axis=-1, keepdims=True)
        if top is None:
            top = mx
        e = jnp.exp(mx - top)
        den = den + e
        idx_out = jnp.where(lane == j, idx, idx_out)
        val_out = jnp.where(lane == j, e, val_out)
        picked = lane_f == idx
        onehot = jnp.where(picked, 1.0, onehot)
        work = jnp.where(picked, NEG_INF, work)
        idxs.append(idx)
    ri_ref[...] = idx_out.astype(jnp.int32)
    rw_ref[...] = val_out / den
    tm = logits.shape[0]
    earlier = (lax.broadcasted_iota(jnp.int32, (tm, tm), 1) < lax.broadcasted_iota(jnp.int32, (tm, tm), 0))
    before = _dot(earlier.astype(BF16), onehot.astype(BF16)) + cnt_ref[0:1, :]
    rank_out = jnp.zeros(logits.shape, F32)
    for j in range(TOP_K):
        rank_j = jnp.sum(jnp.where(lane_f == idxs[j], before, 0.0), axis=-1, keepdims=True)
        rank_out = jnp.where(lane == j, rank_j, rank_out)
    rk_ref[...] = rank_out.astype(jnp.int32)
    cnt_ref[...] = cnt_ref[...] + jnp.sum(onehot, axis=0, keepdims=True)


def _merge_call(rows, n_tiles, ya, yb, yc, gates, x_all, wa, wb, wc, wo, mod3, gffn, wr, br):
    d = x_all.shape[1]
    t = n_tiles * TM
    full = lambda a: pl.BlockSpec(a.shape, lambda i: (0,) * a.ndim)
    row = lambda w: pl.BlockSpec((TM, w), lambda i: (i, 0))
    return pl.pallas_call(
        _merge_kernel,
        grid=(n_tiles,),
        in_specs=[row(ya.shape[1]), row(yb.shape[1]), row(yc.shape[1]), row(gates.shape[1]), row(d),
                  full(wa), full(wb), full(wc), full(wo),
                  rows.mod_spec(2, d), full(gffn), rows.mod_spec(3, d), rows.mod_spec(4, d),
                  full(wr), full(br)],
        out_specs=[row(d), pl.BlockSpec((d // 2 // LANES, TM, LANES), lambda i: (0, i, 0)),
                   row(LANES), row(LANES), row(LANES),
                   pl.BlockSpec((8, LANES), lambda i: (0, 0))],
        out_shape=[jax.ShapeDtypeStruct((t, d), F32), jax.ShapeDtypeStruct((d // 2 // LANES, t, LANES), jnp.uint32),
                   jax.ShapeDtypeStruct((t, LANES), jnp.int32), jax.ShapeDtypeStruct((t, LANES), F32),
                   jax.ShapeDtypeStruct((t, LANES), jnp.int32), jax.ShapeDtypeStruct((8, LANES), F32)],
        compiler_params=_cparams(("arbitrary",)),
        name="merge_router",
    )(ya, yb, yc, gates, x_all, wa, wb, wc, wo, mod3, gffn, mod3, mod3, wr, br)


def _moe_kernel(blk_e_ref, n_used_ref, valid_ref, x_ref, wgu_ref, bgu_ref, wd_ref, bd_ref, o_ref, wgu_s, wd_s):
    i = pl.program_id(0)
    new_expert = (i == 0) | (blk_e_ref[i] != blk_e_ref[jnp.maximum(i - 1, 0)])

    @pl.when(new_expert & (i < n_used_ref[0]))
    def _():
        grp = 2 * LANES
        out_col = lax.broadcasted_iota(jnp.int32, (grp, grp), 1)
        src_col = jnp.where(out_col < LANES, 2 * out_col, 2 * (out_col - LANES) + 1)
        regroup = (lax.broadcasted_iota(jnp.int32, (grp, grp), 0) == src_col).astype(BF16)
        for j in range(wgu_s.shape[1] // grp):
            cols = slice(j * grp, (j + 1) * grp)
            wgu_s[:, cols] = _dot(wgu_ref[0, :, cols].astype(BF16), regroup).astype(BF16)
        wd_s[...] = wd_ref[0].astype(BF16)

    @pl.when(i < n_used_ref[0])
    def _():
        words = [x_ref[c] for c in range(x_ref.shape[0])]
        hi = [lax.bitcast_convert_type(w & jnp.uint32(0xFFFF0000), F32) for w in words]
        lo = [lax.bitcast_convert_type(w << 16, F32) for w in words]
        x = jnp.concatenate(hi + lo, axis=1)
        row = lax.broadcasted_iota(jnp.int32, x.shape, 0)
        x = jnp.where(row < valid_ref[i], x, 0.0).astype(BF16)
        gu = _dot(x, wgu_s[...]) + bgu_ref[0]
        acts = []
        for j in range(gu.shape[1] // (2 * LANES)):
            gate = jnp.minimum(gu[:, 2 * j * LANES:(2 * j + 1) * LANES], SWIGLU_LIMIT)
            up = jnp.clip(gu[:, (2 * j + 1) * LANES:(2 * j + 2) * LANES], -SWIGLU_LIMIT, SWIGLU_LIMIT)
            acts.append((up + 1.0) * (gate * jax.nn.sigmoid(SWIGLU_ALPHA * gate)))
        act = jnp.concatenate(acts, axis=1).astype(BF16)
        y = _dot(act, wd_s[...]) + bd_ref[0]
        width = o_ref.shape[2]
        for c in range(o_ref.shape[0]):
            o_ref[c] = y[:, c * width:(c + 1) * width]

    @pl.when(i >= n_used_ref[0])
    def _():
        o_ref[...] = jnp.zeros_like(o_ref)


def _regroup_gate_up(v):
    lead = v.shape[:-1]
    return jnp.swapaxes(v.reshape(lead + (-1, LANES, 2)), -1, -2).reshape(lead + (-1,))


def _moe_call(xg, blk_e, n_used, blk_valid, wgu, bgu, wd, bd):
    n_pieces, n_rows, _ = xg.shape
    d, f2, f = wgu.shape[1], wgu.shape[2], wd.shape[1]
    n_blk = n_rows // MOE_TB
    last = lambda i, be, nu, bv: jnp.minimum(i, nu[0] - 1)
    grid_spec = pltpu.PrefetchScalarGridSpec(
        num_scalar_prefetch=3,
        grid=(n_blk,),
        in_specs=[pl.BlockSpec((n_pieces, MOE_TB, LANES), lambda i, be, nu, bv: (0, last(i, be, nu, bv), 0)),
                  pl.BlockSpec((1, d, f2), lambda i, be, nu, bv: (be[i], 0, 0)),
                  pl.BlockSpec((1, 1, f2), lambda i, be, nu, bv: (be[i], 0, 0)),
                  pl.BlockSpec((1, f, d), lambda i, be, nu, bv: (be[i], 0, 0)),
                  pl.BlockSpec((1, 1, d), lambda i, be, nu, bv: (be[i], 0, 0))],
        out_specs=pl.BlockSpec((Y_SLABS, MOE_TB, d // Y_SLABS), lambda i, be, nu, bv: (0, i, 0)),
        scratch_shapes=[pltpu.VMEM((d, f2), BF16), pltpu.VMEM((f, d), BF16)],
    )
    return pl.pallas_call(
        _moe_kernel,
        grid_spec=grid_spec,
        out_shape=jax.ShapeDtypeStruct((Y_SLABS, n_rows, d // Y_SLABS), F32),
        compiler_params=_cparams(("arbitrary",)),
        name="moe_experts",
    )(blk_e, n_used, blk_valid, xg, wgu, bgu, wd, bd)


def _combine_kernel(*refs, final):
    x_ref, y_ref, rw_ref, gt2_ref, gfin_ref = refs[:5]
    o_ref = refs[-1]
    rw = rw_ref[...]
    s = None
    for j in range(TOP_K):
        y_j = jnp.concatenate([y_ref[c, j] for c in range(y_ref.shape[0])], axis=1) * rw[:, j:j + 1]
        s = y_j if s is None else s + y_j
    x2 = x_ref[...] + gt2_ref[0] * s
    if final:
        x2 = _rms(x2) * gfin_ref[...]
    o_ref[...] = x2


def _combine_call(rows, n_tiles, first_tile, part_tiles, x1, yg, rw, mod3, gfin, final, prev_out):
    d = x1.shape[1]
    row = lambda w: pl.BlockSpec((TM, w), lambda i: (i + first_tile, 0))
    in_specs = [row(d),
                pl.BlockSpec((Y_SLABS, TOP_K, TM, d // Y_SLABS), lambda i: (0, 0, i, 0)),
                row(LANES),
                rows.mod_spec(5, d, first_tile),
                pl.BlockSpec((1, d), lambda i: (0, 0))]
    args = [x1, yg, rw, mod3, gfin]
    aliases = {}
    if prev_out is not None:
        in_specs.append(pl.BlockSpec(memory_space=pl.ANY))
        args.append(prev_out)
        aliases = {len(args) - 1: 0}
    return pl.pallas_call(
        functools.partial(_combine_kernel, final=final),
        grid=(part_tiles,),
        in_specs=in_specs,
        out_specs=row(d),
        out_shape=jax.ShapeDtypeStruct((n_tiles * TM, d), F32),
        input_output_aliases=aliases,
        compiler_params=_cparams(("parallel",)),
        name="moe_combine_final" if final else "moe_combine",
    )(*args)


def _sc_mesh():
    return plsc.VectorSubcoreMesh(core_axis_name="core", subcore_axis_name="subcore")


def _sc_scatter_rows(src, idx_list, n_out):
    r, c = src.shape
    assert r % SC_WINDOW == 0

    @pl.kernel(out_type=jax.ShapeDtypeStruct((n_out, c), src.dtype), mesh=_sc_mesh(), scratch_types=[])
    def scatter(x_hbm, *refs):
        idx_hbm, o_hbm = refs[:-1], refs[-1]

        def body(x_vmem, *idx_vmem):
            for iv in idx_vmem:
                pltpu.sync_copy(x_vmem, o_hbm.at[iv.at[0]])

        ispec = pl.BlockSpec((1, SC_WINDOW), lambda i: (0, i))
        pltpu.emit_pipeline(
            body, grid=(r // SC_WINDOW,),
            in_specs=[pl.BlockSpec((SC_WINDOW, c), lambda i: (i, 0))] + [ispec] * len(idx_hbm),
            out_specs=[], core_axis_name=("core", "subcore"),
            dimension_semantics=(pltpu.PARALLEL,))(x_hbm, *idx_hbm)

    return scatter(src, *idx_list)


def _sc_gather_rows(src, idx):
    r, c = idx.shape[1], src.shape[1]
    assert r % SC_WINDOW == 0

    @pl.kernel(out_type=jax.ShapeDtypeStruct((r, c), src.dtype), mesh=_sc_mesh(), scratch_types=[])
    def gather(x_hbm, i_hbm, o_hbm):
        def body(i_vmem, o_vmem):
            pltpu.sync_copy(x_hbm.at[i_vmem.at[0]], o_vmem)

        pltpu.emit_pipeline(
            body, grid=(r // SC_WINDOW,),
            in_specs=[pl.BlockSpec((1, SC_WINDOW), lambda i: (0, i))],
            out_specs=[pl.BlockSpec((SC_WINDOW, c), lambda i: (i, 0))],
            core_axis_name=("core", "subcore"),
            dimension_semantics=(pltpu.PARALLEL,))(i_hbm, o_hbm)

    return gather(src, idx)


def _route(top_idx, rank, counts):
    t = top_idx.shape[0]
    m = t * TOP_K
    padded = ((counts + MOE_TB - 1) // MOE_TB) * MOE_TB
    pend = jnp.cumsum(padded)
    pstart = pend - padded
    onehot = top_idx[:, :, None] == jnp.arange(N_EXPERTS, dtype=jnp.int32)[None, None, :]
    dest = jnp.sum(jnp.where(onehot, pstart[None, None, :], 0), axis=-1) + rank
    n_blk = -(-m // MOE_TB) + N_EXPERTS
    n_used = (pend[-1] // MOE_TB).astype(jnp.int32)
    blk = jnp.arange(n_blk, dtype=jnp.int32)
    blk_e = jnp.sum((blk[:, None] * MOE_TB >= pend[None, :]).astype(jnp.int32), axis=1)
    blk_e = jnp.minimum(blk_e, N_EXPERTS - 1)
    e_hot = blk_e[:, None] == jnp.arange(N_EXPERTS, dtype=jnp.int32)[None, :]
    first = jnp.sum(jnp.where(e_hot, pstart[None, :], 0), axis=1) // MOE_TB
    cnt_e = jnp.sum(jnp.where(e_hot, counts[None, :], 0), axis=1)
    blk_valid = jnp.clip(cnt_e - (blk - first) * MOE_TB, 0, MOE_TB).astype(jnp.int32)
    last_e = jnp.sum(jnp.where(blk == n_used - 1, blk_e, 0))
    blk_e = jnp.where(blk < n_used, blk_e, last_e).astype(jnp.int32)
    return dest, blk_e, n_used.reshape(1), blk_valid, n_blk * MOE_TB


def _rope_table(n, rot_dim, lane_off, period):
    qd = rot_dim // 4
    pos = jnp.arange(n, dtype=jnp.int32)
    rws = (pos // GRID_W).astype(F32)
    cls = (pos % GRID_W).astype(F32)
    inv = ROPE_BASE ** (-jnp.arange(qd, dtype=F32) / qd)
    lane = jnp.arange(LANES, dtype=jnp.int32)
    w = (lane - lane_off) % period
    active = (lane >= lane_off) & (w < rot_dim) if period == LANES else jnp.ones((LANES,), bool)
    slot = w // qd
    freq = inv[w % qd]
    ang = jnp.where((slot < 2)[None, :], rws[:, None], cls[:, None]) * freq[None, :]
    sign = jnp.where(slot % 2 == 0, -1.0, 1.0)
    cos = jnp.where(active[None, :], jnp.cos(ang), 1.0)
    sin = jnp.where(active[None, :], jnp.sin(ang) * sign[None, :], 0.0)
    ident = (jnp.ones((TM, LANES), F32), jnp.zeros((TM, LANES), F32))
    return jnp.concatenate([cos, ident[0]], 0), jnp.concatenate([sin, ident[1]], 0)


def _prep_weights(p):
    w_in = p["w_in"]
    nl, d, _ = w_in.shape
    o = {}
    o["w_a"] = w_in[:, :, :3 * A_WIDTH].astype(BF16)
    o["w_b"] = w_in[:, :, 3 * A_WIDTH:3 * A_WIDTH + 3 * B_WIDTH].astype(BF16)
    c0 = 3 * A_WIDTH + 3 * B_WIDTH
    c1 = c0 + C_Q_RANK + C_KV_RANK
    zeros = lambda k: jnp.zeros((nl, d, k), F32)
    o["w_c"] = jnp.concatenate([w_in[:, :, c0:c1], zeros(C_NOPE), w_in[:, :, c1:c1 + C_ROPE],
                                zeros(LANES - C_NOPE - C_ROPE)], axis=-1).astype(BF16)
    o["w_g"] = w_in[:, :, c1 + C_ROPE:].astype(BF16)
    wq = p["w_q_b"].reshape(nl, C_Q_RANK, C_HEADS, C_NOPE + C_ROPE)
    wq = jnp.pad(wq, ((0, 0), (0, 0), (0, 0), (0, LANES - C_NOPE - C_ROPE)))
    o["w_q_b"] = wq.reshape(nl, C_Q_RANK, C_HEADS * LANES).astype(BF16)
    wkv = p["w_kv_b"].reshape(nl, C_KV_RANK, C_HEADS, C_NOPE + C_VDIM)
    wk = jnp.pad(wkv[..., :C_NOPE], ((0, 0), (0, 0), (0, 0), (0, LANES - C_NOPE)))
    o["w_kv_b"] = jnp.concatenate([wk.reshape(nl, C_KV_RANK, C_HEADS * LANES),
                                   wkv[..., C_NOPE:].reshape(nl, C_KV_RANK, C_WIDTH)], axis=-1).astype(BF16)
    for name in ("w_br_a", "w_br_b", "w_br_c", "w_out"):
        o[name] = p[name].astype(BF16)
    o["w_gate_up"] = p["w_gate_up"].reshape((-1,) + p["w_gate_up"].shape[2:])
    o["w_down"] = p["w_down"].reshape((-1,) + p["w_down"].shape[2:])
    o["b_gate_up"] = _regroup_gate_up(p["b_gate_up"]).reshape(-1, 1, p["b_gate_up"].shape[-1])
    o["b_down"] = p["b_down"].reshape(-1, 1, p["b_down"].shape[-1])
    o["w_router"] = jnp.pad(p["w_router"], ((0, 0), (0, 0), (0, LANES - N_EXPERTS)))
    o["b_router"] = jnp.pad(p["b_router"], ((0, 0), (0, LANES - N_EXPERTS)))[:, None, :]
    return o


@jax.jit
def _trunk(p):
    x, ctx = p["x"], p["ctx"]
    bsz, n, d = x.shape
    n_ctx = ctx.shape[1]
    depth = p["w_mod"].shape[0]
    rows = _Rows(bsz, n, n_ctx)
    w = _prep_weights(p)
    rope_a = _rope_table(n, A_HEAD_DIM, 0, A_HEAD_DIM)
    rope_c = _rope_table(n, C_ROPE, C_NOPE, LANES)
    scale_a = A_HEAD_DIM ** -0.5 * LOG2E
    scale_b = B_HEAD_DIM ** -0.5 * LOG2E
    scale_c = (C_NOPE + C_ROPE) ** -0.5 * LOG2E
    c8 = jnp.zeros((8, d), F32).at[:bsz].set(p["c"]).at[bsz].set(p["c_ctx"])
    x_all = jnp.concatenate([x.reshape(bsz * n, d), ctx.reshape(bsz * n_ctx, d)], axis=0)
    dummy_lam = jnp.zeros((4, A_HEAD_DIM), F32)
    dummy_g = jnp.ones((1, LANES), F32)
    out = None
    for l in range(depth):
        last = l == depth - 1
        lam_init = 0.8 - 0.6 * math.exp(-0.3 * l)
        mod3 = _mod_call(c8, p["w_mod"][l], p["b_mod"][l]).reshape(8 * N_MOD, 1, d)
        g_mix = p["g_mix"][l][None, :]
        qa, ka, va = _inproj_qkv_call(rows, x_all, g_mix, mod3, w["w_a"][l], rope_a, scale_a)
        qb, kb, vb = _inproj_qkv_call(rows, x_all, g_mix, mod3, w["w_b"][l], None, scale_b)
        qc, kc, vc = _inproj_mla_call(rows, x_all, g_mix, mod3, w["w_c"][l], p["g_q_a"][l][None, :],
                                      w["w_q_b"][l], p["g_kv_a"][l][None, :], w["w_kv_b"][l], rope_c, scale_c)
        gates = _inproj_gates_call(rows, x_all, g_mix, mod3, w["w_g"][l])
        lamv = jnp.stack([p["lam_q1"][l], p["lam_k1"][l], p["lam_q2"][l], p["lam_k2"][l]])
        gs = p["g_subln"][l][None, :]
        attn = functools.partial(_pair_attention_call, rows)
        ya = attn(qa, ka, va, lamv, gs, wq=LANES, mode="diff", lam_init=lam_init, ctx_queries=False)
        yb = _nbr_attention_call(rows, qb, kb, vb, p["rpb"][l])
        yc = attn(qc, kc, vc, dummy_lam, dummy_g, wq=2 * LANES, mode="pair", lam_init=0.0, ctx_queries=False)
        if not last:
            ya_c = attn(qa, ka, va, lamv, gs, wq=LANES, mode="diff", lam_init=lam_init, ctx_queries=True)
            yb_c = attn(qb, kb, vb, dummy_lam, dummy_g, wq=LANES, mode="pair", lam_init=0.0, ctx_queries=True)
            yc_c = attn(qc, kc, vc, dummy_lam, dummy_g, wq=2 * LANES, mode="pair", lam_init=0.0, ctx_queries=True)
            ya, yb, yc = (jnp.concatenate(pair, axis=0) for pair in ((ya, ya_c), (yb, yb_c), (yc, yc_c)))
        n_tiles = rows.lat_tiles if last else rows.all_tiles
        x1, h2, ri, rw, rk, cnt = _merge_call(rows, n_tiles, ya, yb, yc, gates, x_all, w["w_br_a"][l], w["w_br_b"][l],
                                     w["w_br_c"][l], w["w_out"][l], mod3, p["g_ffn"][l][None, :],
                                     w["w_router"][l], w["b_router"][l])
        dest, blk_e, n_used, blk_valid, n_rows = _route(ri[:, :TOP_K], rk[:, :TOP_K],
                                                        cnt[0, :N_EXPERTS].astype(jnp.int32))
        t_rows = n_tiles * TM
        n_pieces = h2.shape[0]
        piece_off = (jnp.arange(n_pieces, dtype=jnp.int32) * n_rows)[:, None]
        xg = _sc_scatter_rows(h2.reshape(n_pieces * t_rows, LANES),
                              [(piece_off + dest[None, :, j]).reshape(1, -1) for j in range(TOP_K)],
                              n_pieces * n_rows).reshape(n_pieces, n_rows, LANES)
        y = _moe_call(xg, blk_e + l * N_EXPERTS, n_used, blk_valid, w["w_gate_up"], w["b_gate_up"],
                      w["w_down"], w["b_down"])
        slab_off = (jnp.arange(Y_SLABS, dtype=jnp.int32) * n_rows)[:, None, None]
        assert n_tiles % COMBINE_PARTS == 0
        part_tiles = n_tiles // COMBINE_PARTS
        x_new = None
        for k in range(COMBINE_PARTS):
            tok = slice(k * part_tiles * TM, (k + 1) * part_tiles * TM)
            yg = _sc_gather_rows(y.reshape(Y_SLABS * n_rows, d // Y_SLABS),
                                 (slab_off + dest.T[None, :, tok]).reshape(1, -1))
            yg = yg.reshape(Y_SLABS, TOP_K, part_tiles * TM, d // Y_SLABS)
            x_new = _combine_call(rows, n_tiles, k * part_tiles, part_tiles, x1, yg, rw, mod3,
                                  p["g_final"][None, :], last, x_new)
        if last:
            out = x_new.reshape(bsz, n, d)
        else:
            x_all = x_new
    return out


def kernel(x, c, ctx, c_ctx, w_mod, b_mod, g_mix, w_in, lam_q1, lam_k1, lam_q2, lam_k2, g_subln, rpb, g_q_a, w_q_b, g_kv_a, w_kv_b, w_br_a, w_br_b, w_br_c, w_out, g_ffn, w_router, b_router, w_gate_up, b_gate_up, w_down, b_down, g_final):
    return _trunk(dict(x=x, c=c, ctx=ctx, c_ctx=c_ctx, w_mod=w_mod, b_mod=b_mod, g_mix=g_mix, w_in=w_in,
                       lam_q1=lam_q1, lam_k1=lam_k1, lam_q2=lam_q2, lam_k2=lam_k2, g_subln=g_subln, rpb=rpb,
                       g_q_a=g_q_a, w_q_b=w_q_b, g_kv_a=g_kv_a, w_kv_b=w_kv_b, w_br_a=w_br_a, w_br_b=w_br_b,
                       w_br_c=w_br_c, w_out=w_out, g_ffn=g_ffn, w_router=w_router, b_router=b_router,
                       w_gate_up=w_gate_up, b_gate_up=b_gate_up, w_down=w_down, b_down=b_down, g_final=g_final))
```

```python
import functools
import math

import jax
import jax.numpy as jnp
from jax import lax
from jax.experimental import pallas as pl
from jax.experimental.pallas import tpu as pltpu
from jax.experimental.pallas import tpu_sc as plsc

GRID_W = 64
A_HEADS = 4
A_HEAD_DIM = 64
A_WIDTH = A_HEADS * 2 * A_HEAD_DIM
B_HEADS = 8
B_HEAD_DIM = 64
B_WIDTH = B_HEADS * B_HEAD_DIM
NA_ROWS = 8
NA_COLS = 16
C_HEADS = 8
C_NOPE = 64
C_ROPE = 32
C_VDIM = 64
C_Q_RANK = 384
C_KV_RANK = 256
C_WIDTH = C_HEADS * C_VDIM
N_BRANCH = 3
N_EXPERTS = 32
TOP_K = 4
SWIGLU_LIMIT = 7.0
SWIGLU_ALPHA = 1.702
N_MOD = 6
ROPE_BASE = 10000.0
EPS = 1e-6
NEG_INF = -1e30
LOG2E = 1.4426950408889634

LANES = 128
VMEM_LIMIT = 56 * 1024 * 1024
TM = 512
TQ = 512
TK = 2048
MOE_TB = 512
Y_SLABS = 4
SC_WINDOW = 128
NB_GROUP = 4
NB_GROUPS_PER_STEP = 16

BF16 = jnp.bfloat16
F32 = jnp.float32


def _cparams(sem):
    return pltpu.CompilerParams(dimension_semantics=sem, vmem_limit_bytes=VMEM_LIMIT)


def _dot(a, b):
    return jnp.dot(a, b, preferred_element_type=F32)


def _dot_nt(a, b):
    return lax.dot_general(a, b, (((1,), (1,)), ((), ())), preferred_element_type=F32)


def _split_bf16(a):
    hi = a.astype(BF16)
    lo = (a - hi.astype(F32)).astype(BF16)
    return hi, lo


def _dot_split(a, b):
    a_hi, a_lo = _split_bf16(a)
    b_hi, b_lo = _split_bf16(b)
    return _dot(a_hi, b_hi) + _dot(a_hi, b_lo) + _dot(a_lo, b_hi)


def _rms(x):
    return x * lax.rsqrt(jnp.mean(x * x, axis=-1, keepdims=True) + EPS)


def _modulated(x_ref, g_ref, sh_ref, sc_ref):
    return (_rms(x_ref[...]) * g_ref[...]) * (1.0 + sc_ref[0]) + sh_ref[0]


def _rope(blk, cos, sin_signed, shift):
    lane = lax.broadcasted_iota(jnp.int32, blk.shape, 1)
    first = ((lane // shift) % 2) == 0
    partner = jnp.where(first, pltpu.roll(blk, LANES - shift, 1), pltpu.roll(blk, shift, 1))
    return blk * cos + partner * sin_signed


def _mod_kernel(c_ref, w_ref, b_ref, o_ref):
    cc = c_ref[...]
    a = cc * jax.nn.sigmoid(cc)
    o_ref[...] = _dot_split(a, w_ref[...]) + b_ref[...]


def _mod_call(c8, w_mod, b_mod):
    d = c8.shape[1]
    n_out = w_mod.shape[1]
    return pl.pallas_call(
        _mod_kernel,
        grid=(n_out // d,),
        in_specs=[pl.BlockSpec((8, d), lambda j: (0, 0)),
                  pl.BlockSpec((d, d), lambda j: (0, j)),
                  pl.BlockSpec((1, d), lambda j: (0, j))],
        out_specs=pl.BlockSpec((8, d), lambda j: (0, j)),
        out_shape=jax.ShapeDtypeStruct((8, n_out), F32),
        compiler_params=_cparams(("arbitrary",)),
        name="mod",
    )(c8, w_mod, b_mod.reshape(1, n_out))


class _Rows:
    def __init__(self, bsz, n, n_ctx):
        assert n % TM == 0 and (bsz * n_ctx) % TM == 0
        self.bsz, self.n, self.n_ctx = bsz, n, n_ctx
        self.t_lat = bsz * n
        self.t_all = bsz * n + bsz * n_ctx
        self.tiles_per_batch = n // TM
        self.lat_tiles = self.t_lat // TM
        self.all_tiles = self.t_all // TM

    def mod_spec(self, which, d):
        tpb, bsz = self.tiles_per_batch, self.bsz
        return pl.BlockSpec((1, 1, d), lambda i, *_: (jnp.minimum(i // tpb, bsz) * N_MOD + which, 0, 0))

    def rope_spec(self):
        tpb, lat = self.tiles_per_batch, self.lat_tiles
        return pl.BlockSpec((TM, LANES), lambda i, *_: (jnp.where(i < lat, i % tpb, tpb), 0))


def _qkv_epilogue(t, outs, tables, qscale):
    per = outs[0].shape[1] // LANES
    for j in range(3 * per):
        blk = t[:, j * LANES:(j + 1) * LANES]
        if tables is not None and j < 2 * per:
            blk = _rope(blk, tables[0][...], tables[1][...], A_HEAD_DIM // 4)
        if j < per:
            blk = blk * qscale
        outs[j // per][:, (j % per) * LANES:(j % per + 1) * LANES] = blk.astype(BF16)


def _mla_epilogue(t, gq_ref, wq_ref, gkv_ref, wkv_ref, cos_ref, sin_ref, q_ref, k_ref, v_ref, qscale):
    cos, sin = cos_ref[...], sin_ref[...]
    shift = C_ROPE // 4
    cq = (_rms(t[:, :C_Q_RANK]) * gq_ref[...]).astype(BF16)
    q = _dot(cq, wq_ref[...])
    for j in range(C_HEADS):
        blk = _rope(q[:, j * LANES:(j + 1) * LANES], cos, sin, shift) * qscale
        q_ref[:, j * LANES:(j + 1) * LANES] = blk.astype(BF16)
    ckv = (_rms(t[:, C_Q_RANK:C_Q_RANK + C_KV_RANK]) * gkv_ref[...]).astype(BF16)
    kv = _dot(ckv, wkv_ref[...])
    pe = _rope(t[:, C_Q_RANK + C_KV_RANK:], cos, sin, shift)
    for j in range(C_HEADS):
        k_ref[:, j * LANES:(j + 1) * LANES] = (kv[:, j * LANES:(j + 1) * LANES] + pe).astype(BF16)
    v_ref[...] = kv[:, C_HEADS * LANES:].astype(BF16)


def _inproj_attn_kernel(x_ref, g_ref, sh_ref, sc_ref, wa_ref, wb_ref, wc_ref, gq_ref, wq_ref, gkv_ref, wkv_ref,
                        cosa_ref, sina_ref, cosc_ref, sinc_ref,
                        qa_ref, ka_ref, va_ref, qb_ref, kb_ref, vb_ref, qc_ref, kc_ref, vc_ref, *, scales):
    h = _modulated(x_ref, g_ref, sh_ref, sc_ref).astype(BF16)
    _qkv_epilogue(_dot(h, wa_ref[...]), (qa_ref, ka_ref, va_ref), (cosa_ref, sina_ref), scales[0])
    _qkv_epilogue(_dot(h, wb_ref[...]), (qb_ref, kb_ref, vb_ref), None, scales[1])
    _mla_epilogue(_dot(h, wc_ref[...]), gq_ref, wq_ref, gkv_ref, wkv_ref, cosc_ref, sinc_ref,
                  qc_ref, kc_ref, vc_ref, scales[2])


def _inproj_attn_call(rows, x_all, g, mod3, wa, wb, wc, gq, wq, gkv, wkv, rope_a, rope_c, scales):
    d = x_all.shape[1]
    full = lambda a: pl.BlockSpec(a.shape, lambda i: (0,) * a.ndim)
    row = lambda w: pl.BlockSpec((TM, w), lambda i: (i, 0))
    widths = [A_WIDTH] * 3 + [B_WIDTH] * 3 + [C_HEADS * LANES] * 2 + [C_WIDTH]
    return pl.pallas_call(
        functools.partial(_inproj_attn_kernel, scales=scales),
        grid=(rows.all_tiles,),
        in_specs=[row(d), full(g), rows.mod_spec(0, d), rows.mod_spec(1, d),
                  full(wa), full(wb), full(wc), full(gq), full(wq), full(gkv), full(wkv),
                  rows.rope_spec(), rows.rope_spec(), rows.rope_spec(), rows.rope_spec()],
        out_specs=[row(w) for w in widths],
        out_shape=[jax.ShapeDtypeStruct((rows.t_all, w), BF16) for w in widths],
        compiler_params=_cparams(("parallel",)),
        name="inproj_attn",
    )(x_all, g, mod3, mod3, wa, wb, wc, gq, wq, gkv, wkv, *rope_a, *rope_c)


def _inproj_gates_kernel(x_ref, g_ref, sh_ref, sc_ref, w_ref, o_ref):
    h = _modulated(x_ref, g_ref, sh_ref, sc_ref).astype(BF16)
    d = x_ref.shape[1]
    for j in range(o_ref.shape[1] // d):
        o_ref[:, j * d:(j + 1) * d] = jax.nn.sigmoid(_dot(h, w_ref[:, j * d:(j + 1) * d])).astype(BF16)


def _inproj_gates_call(rows, x_all, g, mod3, w):
    d = x_all.shape[1]
    n_out = w.shape[1]
    return pl.pallas_call(
        _inproj_gates_kernel,
        grid=(rows.all_tiles,),
        in_specs=[pl.BlockSpec((TM, d), lambda i: (i, 0)),
                  pl.BlockSpec((1, d), lambda i: (0, 0)),
                  rows.mod_spec(0, d), rows.mod_spec(1, d),
                  pl.BlockSpec((d, n_out), lambda i: (0, 0))],
        out_specs=pl.BlockSpec((TM, n_out), lambda i: (i, 0)),
        out_shape=jax.ShapeDtypeStruct((rows.t_all, n_out), BF16),
        compiler_params=_cparams(("parallel",)),
        name="inproj_gates",
    )(x_all, g, mod3, mod3, w)


def _pair_attention_kernel(*refs, n_lat_chunks, tk, mode, lam_init):
    if n_lat_chunks:
        q_ref, kl_ref, vl_ref, kc_ref, vc_ref, lam_ref, gs_ref, o_ref = refs
    else:
        q_ref, kc_ref, vc_ref, lam_ref, gs_ref, o_ref = refs
    q = q_ref[...]
    tq, wq = q.shape
    lane = lax.broadcasted_iota(jnp.int32, q.shape, 1)
    lower = lane < wq // 2
    zero = jnp.zeros_like(q)
    qs = jnp.concatenate([jnp.where(lower, q, zero), jnp.where(lower, zero, q)], axis=0)

    def step(kc, vc, carry):
        m, l, acc = carry
        s = _dot_nt(qs, kc)
        m_new = jnp.maximum(m, jnp.max(s, axis=-1, keepdims=True))
        p = jnp.exp2(s - m_new)
        alpha = jnp.exp2(m - m_new)
        l = alpha * l + jnp.sum(p, axis=-1, keepdims=True)
        acc = alpha * acc + _dot(p.astype(BF16), vc)
        return m_new, l, acc

    carry = (jnp.full((2 * tq, 1), NEG_INF, F32), jnp.zeros((2 * tq, 1), F32),
             jnp.zeros((2 * tq, LANES), F32))
    for i in range(n_lat_chunks):
        carry = step(kl_ref[i * tk:(i + 1) * tk, :], vl_ref[i * tk:(i + 1) * tk, :], carry)
    m, l, acc = step(kc_ref[...], vc_ref[...], carry)
    o = acc / l
    o0, o1 = o[:tq], o[tq:]
    if mode == "diff":
        lv = lam_ref[...]
        lam = (jnp.exp(jnp.sum(lv[0:1] * lv[1:2], axis=-1, keepdims=True))
               - jnp.exp(jnp.sum(lv[2:3] * lv[3:4], axis=-1, keepdims=True)) + lam_init)
        out = (_rms(o0 - lam * o1) * gs_ref[...]) * (1.0 - lam_init)
    else:
        out = jnp.where(lax.broadcasted_iota(jnp.int32, o0.shape, 1) < LANES // 2, o0, o1)
    o_ref[...] = out.astype(BF16)


def _pair_attention_call(rows, q, k, v, lamv, gs, *, wq, mode, lam_init, ctx_queries):
    bsz, n, n_ctx = rows.bsz, rows.n, rows.n_ctx
    n_pairs = v.shape[1] // LANES
    ctx_blk0 = rows.t_lat // n_ctx
    kc_spec = pl.BlockSpec((n_ctx, wq), lambda b, h, i: (ctx_blk0 + b, h))
    vc_spec = pl.BlockSpec((n_ctx, LANES), lambda b, h, i: (ctx_blk0 + b, h))
    small = [pl.BlockSpec(lamv.shape, lambda b, h, i: (0, 0)), pl.BlockSpec(gs.shape, lambda b, h, i: (0, 0))]
    if ctx_queries:
        tq, n_q, n_chunks, tk = n_ctx, 1, 0, 0
        q_spec = pl.BlockSpec((tq, wq), lambda b, h, i: (ctx_blk0 + b, h))
        o_spec = pl.BlockSpec((tq, LANES), lambda b, h, i: (b, h))
        in_specs = [q_spec, kc_spec, vc_spec] + small
        args = (q, k, v, lamv, gs)
        out_rows = bsz * n_ctx
    else:
        tk = min(TK, n)
        tq = min(TQ, n)
        assert n % tq == 0 and n % tk == 0
        n_q, n_chunks = n // tq, n // tk
        q_spec = pl.BlockSpec((tq, wq), lambda b, h, i: (b * n_q + i, h))
        o_spec = pl.BlockSpec((tq, LANES), lambda b, h, i: (b * n_q + i, h))
        in_specs = [q_spec,
                    pl.BlockSpec((n, wq), lambda b, h, i: (b, h)),
                    pl.BlockSpec((n, LANES), lambda b, h, i: (b, h)),
                    kc_spec, vc_spec] + small
        args = (q, k, v, k, v, lamv, gs)
        out_rows = rows.t_lat
    return pl.pallas_call(
        functools.partial(_pair_attention_kernel, n_lat_chunks=n_chunks, tk=tk, mode=mode, lam_init=lam_init),
        grid=(bsz, n_pairs, n_q),
        in_specs=in_specs,
        out_specs=o_spec,
        out_shape=jax.ShapeDtypeStruct((out_rows, n_pairs * LANES), BF16),
        compiler_params=_cparams(("parallel", "parallel", "arbitrary")),
        name=f"attn_{mode}_{'ctx' if ctx_queries else 'lat'}_{wq}",
    )(*args)


def _nbr_groups(n_rows):
    half = NA_ROWS // 2
    span = NA_ROWS + NB_GROUP - 1
    starts, classes, patterns = [], [], []
    for r0 in range(0, n_rows, NB_GROUP):
        rs = [min(max(r - half, 0), n_rows - NA_ROWS) for r in range(r0, r0 + NB_GROUP)]
        us = min(rs[0], n_rows - span)
        pat = (tuple(x - us for x in rs), tuple(r - us for r in range(r0, r0 + NB_GROUP)))
        if pat not in patterns:
            patterns.append(pat)
        starts.append(us)
        classes.append(patterns.index(pat))
    return starts, classes, patterns


def _nbr_bias_table(rpb, patterns):
    span = NA_ROWS + NB_GROUP - 1
    c_idx = jnp.arange(GRID_W, dtype=jnp.int32)
    col_start = jnp.clip(c_idx - NA_COLS // 2, 0, GRID_W - NA_COLS)
    col_mask = (c_idx[None, :] >= col_start[:, None]) & (c_idx[None, :] < col_start[:, None] + NA_COLS)
    col_bias_idx = jnp.clip(c_idx[None, :] - c_idx[:, None] + NA_COLS - 1, 0, 2 * NA_COLS - 2)
    rpb_c = rpb[:, :, col_bias_idx] * LOG2E
    u = jnp.arange(span)
    tabs = []
    for rs_rel, r_rel in patterns:
        per_row = []
        for a in range(NB_GROUP):
            valid = (u >= rs_rel[a]) & (u < rs_rel[a] + NA_ROWS)
            di = jnp.clip(u - r_rel[a] + NA_ROWS - 1, 0, 2 * NA_ROWS - 2)
            bias = jnp.transpose(rpb_c[:, di], (0, 2, 1, 3))
            ok = valid[None, None, :, None] & col_mask[None, :, None, :]
            per_row.append(jnp.where(ok, bias, NEG_INF).reshape(rpb.shape[0], GRID_W, span * GRID_W))
        tabs.append(jnp.concatenate(per_row, axis=1))
    return jnp.stack(tabs).astype(F32)


def _nbr_attention_kernel(us_ref, cls_ref, q_ref, kl_ref, vl_ref, kc_ref, vc_ref, bias_ref, o_ref, *, groups_per_step):
    kc, vc = kc_ref[...], vc_ref[...]
    rows_q = NB_GROUP * GRID_W
    win = (NA_ROWS + NB_GROUP - 1) * GRID_W

    def one_group(j, _):
        g = pl.program_id(2) * groups_per_step + j
        q = q_ref[pl.ds(pl.multiple_of(j * rows_q, rows_q), rows_q), :]
        lane = lax.broadcasted_iota(jnp.int32, q.shape, 1)
        lower = lane < LANES // 2
        zero = jnp.zeros_like(q)
        qs = jnp.concatenate([jnp.where(lower, q, zero), jnp.where(lower, zero, q)], axis=0)
        off = pl.multiple_of(us_ref[g] * GRID_W, GRID_W)
        kw = kl_ref[pl.ds(off, win), :]
        vw = vl_ref[pl.ds(off, win), :]
        bias = bias_ref[cls_ref[g]].reshape(2 * rows_q, win)
        s = jnp.where(bias > 0.5 * NEG_INF, _dot_nt(qs, kw) + bias, NEG_INF)
        s_ctx = _dot_nt(qs, kc)
        m = jnp.maximum(jnp.max(s, axis=-1, keepdims=True), jnp.max(s_ctx, axis=-1, keepdims=True))
        p = jnp.exp2(s - m)
        p_ctx = jnp.exp2(s_ctx - m)
        l = jnp.sum(p, axis=-1, keepdims=True) + jnp.sum(p_ctx, axis=-1, keepdims=True)
        o = (_dot(p.astype(BF16), vw) + _dot(p_ctx.astype(BF16), vc)) / l
        out = jnp.where(lax.broadcasted_iota(jnp.int32, (rows_q, LANES), 1) < LANES // 2,
                        o[:rows_q], o[rows_q:])
        o_ref[pl.ds(pl.multiple_of(j * rows_q, rows_q), rows_q), :] = out.astype(BF16)
        return 0

    lax.fori_loop(0, groups_per_step, one_group, 0, unroll=True)


def _nbr_attention_call(rows, q, k, v, rpb):
    bsz, n, n_ctx = rows.bsz, rows.n, rows.n_ctx
    n_rows = n // GRID_W
    groups_per_step = min(NB_GROUPS_PER_STEP, n_rows // NB_GROUP)
    rows_per_step = NB_GROUP * groups_per_step
    assert n_rows % rows_per_step == 0 and n_rows >= NA_ROWS + NB_GROUP - 1
    starts, classes, patterns = _nbr_groups(n_rows)
    bias = _nbr_bias_table(rpb, patterns)
    steps = n_rows // rows_per_step
    n_pairs = B_HEADS // 2
    ctx_blk0 = rows.t_lat // n_ctx
    tq = rows_per_step * GRID_W
    grid_spec = pltpu.PrefetchScalarGridSpec(
        num_scalar_prefetch=2,
        grid=(bsz, n_pairs, steps),
        in_specs=[pl.BlockSpec((tq, LANES), lambda b, h, i, us, cl: (b * steps + i, h)),
                  pl.BlockSpec((n, LANES), lambda b, h, i, us, cl: (b, h)),
                  pl.BlockSpec((n, LANES), lambda b, h, i, us, cl: (b, h)),
                  pl.BlockSpec((n_ctx, LANES), lambda b, h, i, us, cl: (ctx_blk0 + b, h)),
                  pl.BlockSpec((n_ctx, LANES), lambda b, h, i, us, cl: (ctx_blk0 + b, h)),
                  pl.BlockSpec((bias.shape[0], 2) + bias.shape[2:], lambda b, h, i, us, cl: (0, h, 0, 0))],
        out_specs=pl.BlockSpec((tq, LANES), lambda b, h, i, us, cl: (b * steps + i, h)),
    )
    return pl.pallas_call(
        functools.partial(_nbr_attention_kernel, groups_per_step=groups_per_step),
        grid_spec=grid_spec,
        out_shape=jax.ShapeDtypeStruct((rows.t_lat, B_WIDTH), BF16),
        compiler_params=_cparams(("parallel", "parallel", "arbitrary")),
        name="attn_nbr",
    )(jnp.asarray(starts, jnp.int32), jnp.asarray(classes, jnp.int32), q, k, v, k, v, bias)


def _merge_kernel(ya_ref, yb_ref, yc_ref, g_ref, x_ref, wa_ref, wb_ref, wc_ref, wo_ref,
                  gt1_ref, gffn_ref, sh2_ref, sc2_ref, wr_ref, br_ref,
                  x1_ref, h2_ref, ri_ref, rw_ref, rk_ref, cnt_ref):
    d = x_ref.shape[1]

    @pl.when(pl.program_id(0) == 0)
    def _():
        cnt_ref[...] = jnp.zeros_like(cnt_ref)

    g = g_ref[...].astype(F32)
    m = (g[:, :d] * _dot(ya_ref[...], wa_ref[...])
         + g[:, d:2 * d] * _dot(yb_ref[...], wb_ref[...])
         + g[:, 2 * d:] * _dot(yc_ref[...], wc_ref[...]))
    y = _dot(m.astype(BF16), wo_ref[...])
    x1 = x_ref[...] + gt1_ref[0] * y
    x1_ref[...] = x1
    h2 = (_rms(x1) * gffn_ref[...]) * (1.0 + sc2_ref[0]) + sh2_ref[0]
    hb = lax.bitcast_convert_type(h2.astype(BF16).astype(F32), jnp.uint32)
    words = hb[:, :d // 2] | (hb[:, d // 2:] >> 16)
    for c in range(h2_ref.shape[0]):
        h2_ref[c] = words[:, c * LANES:(c + 1) * LANES]
    logits = _dot_split(h2, wr_ref[...]) + br_ref[...]
    lane = lax.broadcasted_iota(jnp.int32, logits.shape, 1)
    lane_f = lane.astype(F32)
    work = jnp.where(lane < N_EXPERTS, logits, NEG_INF)
    idx_out = jnp.zeros(logits.shape, F32)
    val_out = jnp.zeros(logits.shape, F32)
    onehot = jnp.zeros(logits.shape, F32)
    top = None
    den = jnp.zeros((logits.shape[0], 1), F32)
    idxs = []
    for j in range(TOP_K):
        mx = jnp.max(work, axis=-1, keepdims=True)
        idx = jnp.min(jnp.where(work == mx, lane_f, float(LANES)), axis=-1, keepdims=True)
        if top is None:
            top = mx
        e = jnp.exp(mx - top)
        den = den + e
        idx_out = jnp.where(lane == j, idx, idx_out)
        val_out = jnp.where(lane == j, e, val_out)
        picked = lane_f == idx
        onehot = jnp.where(picked, 1.0, onehot)
        work = jnp.where(picked, NEG_INF, work)
        idxs.append(idx)
    ri_ref[...] = idx_out.astype(jnp.int32)
    rw_ref[...] = val_out / den
    tm = logits.shape[0]
    earlier = (lax.broadcasted_iota(jnp.int32, (tm, tm), 1) < lax.broadcasted_iota(jnp.int32, (tm, tm), 0))
    before = _dot(earlier.astype(BF16), onehot.astype(BF16)) + cnt_ref[0:1, :]
    rank_out = jnp.zeros(logits.shape, F32)
    for j in range(TOP_K):
        rank_j = jnp.sum(jnp.where(lane_f == idxs[j], before, 0.0), axis=-1, keepdims=True)
        rank_out = jnp.where(lane == j, rank_j, rank_out)
    rk_ref[...] = rank_out.astype(jnp.int32)
    cnt_ref[...] = cnt_ref[...] + jnp.sum(onehot, axis=0, keepdims=True)


def _merge_call(rows, n_tiles, ya, yb, yc, gates, x_all, wa, wb, wc, wo, mod3, gffn, wr, br):
    d = x_all.shape[1]
    t = n_tiles * TM
    full = lambda a: pl.BlockSpec(a.shape, lambda i: (0,) * a.ndim)
    row = lambda w: pl.BlockSpec((TM, w), lambda i: (i, 0))
    return pl.pallas_call(
        _merge_kernel,
        grid=(n_tiles,),
        in_specs=[row(ya.shape[1]), row(yb.shape[1]), row(yc.shape[1]), row(gates.shape[1]), row(d),
                  full(wa), full(wb), full(wc), full(wo),
                  rows.mod_spec(2, d), full(gffn), rows.mod_spec(3, d), rows.mod_spec(4, d),
                  full(wr), full(br)],
        out_specs=[row(d), pl.BlockSpec((d // 2 // LANES, TM, LANES), lambda i: (0, i, 0)),
                   row(LANES), row(LANES), row(LANES),
                   pl.BlockSpec((8, LANES), lambda i: (0, 0))],
        out_shape=[jax.ShapeDtypeStruct((t, d), F32), jax.ShapeDtypeStruct((d // 2 // LANES, t, LANES), jnp.uint32),
                   jax.ShapeDtypeStruct((t, LANES), jnp.int32), jax.ShapeDtypeStruct((t, LANES), F32),
                   jax.ShapeDtypeStruct((t, LANES), jnp.int32), jax.ShapeDtypeStruct((8, LANES), F32)],
        compiler_params=_cparams(("arbitrary",)),
        name="merge_router",
    )(ya, yb, yc, gates, x_all, wa, wb, wc, wo, mod3, gffn, mod3, mod3, wr, br)


def _moe_kernel(blk_e_ref, n_used_ref, valid_ref, x_ref, wgu_ref, bgu_ref, wd_ref, bd_ref, o_ref, wgu_s, wd_s):
    i = pl.program_id(0)
    new_expert = (i == 0) | (blk_e_ref[i] != blk_e_ref[jnp.maximum(i - 1, 0)])

    @pl.when(new_expert & (i < n_used_ref[0]))
    def _():
        grp = 2 * LANES
        out_col = lax.broadcasted_iota(jnp.int32, (grp, grp), 1)
        src_col = jnp.where(out_col < LANES, 2 * out_col, 2 * (out_col - LANES) + 1)
        regroup = (lax.broadcasted_iota(jnp.int32, (grp, grp), 0) == src_col).astype(BF16)
        for j in range(wgu_s.shape[1] // grp):
            cols = slice(j * grp, (j + 1) * grp)
            wgu_s[:, cols] = _dot(wgu_ref[0, :, cols].astype(BF16), regroup).astype(BF16)
        wd_s[...] = wd_ref[0].astype(BF16)

    @pl.when(i < n_used_ref[0])
    def _():
        words = [x_ref[c] for c in range(x_ref.shape[0])]
        hi = [lax.bitcast_convert_type(w & jnp.uint32(0xFFFF0000), F32) for w in words]
        lo = [lax.bitcast_convert_type(w << 16, F32) for w in words]
        x = jnp.concatenate(hi + lo, axis=1)
        row = lax.broadcasted_iota(jnp.int32, x.shape, 0)
        x = jnp.where(row < valid_ref[i], x, 0.0).astype(BF16)
        gu = _dot(x, wgu_s[...]) + bgu_ref[0]
        acts = []
        for j in range(gu.shape[1] // (2 * LANES)):
            gate = jnp.minimum(gu[:, 2 * j * LANES:(2 * j + 1) * LANES], SWIGLU_LIMIT)
            up = jnp.clip(gu[:, (2 * j + 1) * LANES:(2 * j + 2) * LANES], -SWIGLU_LIMIT, SWIGLU_LIMIT)
            acts.append((up + 1.0) * (gate * jax.nn.sigmoid(SWIGLU_ALPHA * gate)))
        act = jnp.concatenate(acts, axis=1).astype(BF16)
        y = _dot(act, wd_s[...]) + bd_ref[0]
        width = o_ref.shape[2]
        for c in range(o_ref.shape[0]):
            o_ref[c] = y[:, c * width:(c + 1) * width]

    @pl.when(i >= n_used_ref[0])
    def _():
        o_ref[...] = jnp.zeros_like(o_ref)


def _regroup_gate_up(v):
    lead = v.shape[:-1]
    return jnp.swapaxes(v.reshape(lead + (-1, LANES, 2)), -1, -2).reshape(lead + (-1,))


def _moe_call(xg, blk_e, n_used, blk_valid, wgu, bgu, wd, bd):
    n_pieces, n_rows, _ = xg.shape
    d, f2, f = wgu.shape[1], wgu.shape[2], wd.shape[1]
    n_blk = n_rows // MOE_TB
    last = lambda i, be, nu, bv: jnp.minimum(i, nu[0] - 1)
    grid_spec = pltpu.PrefetchScalarGridSpec(
        num_scalar_prefetch=3,
        grid=(n_blk,),
        in_specs=[pl.BlockSpec((n_pieces, MOE_TB, LANES), lambda i, be, nu, bv: (0, last(i, be, nu, bv), 0)),
                  pl.BlockSpec((1, d, f2), lambda i, be, nu, bv: (be[i], 0, 0)),
                  pl.BlockSpec((1, 1, f2), lambda i, be, nu, bv: (be[i], 0, 0)),
                  pl.BlockSpec((1, f, d), lambda i, be, nu, bv: (be[i], 0, 0)),
                  pl.BlockSpec((1, 1, d), lambda i, be, nu, bv: (be[i], 0, 0))],
        out_specs=pl.BlockSpec((Y_SLABS, MOE_TB, d // Y_SLABS), lambda i, be, nu, bv: (0, i, 0)),
        scratch_shapes=[pltpu.VMEM((d, f2), BF16), pltpu.VMEM((f, d), BF16)],
    )
    return pl.pallas_call(
        _moe_kernel,
        grid_spec=grid_spec,
        out_shape=jax.ShapeDtypeStruct((Y_SLABS, n_rows, d // Y_SLABS), F32),
        compiler_params=_cparams(("arbitrary",)),
        name="moe_experts",
    )(blk_e, n_used, blk_valid, xg, wgu, bgu, wd, bd)


def _combine_kernel(x_ref, y_ref, rw_ref, gt2_ref, gfin_ref, o_ref, *, final):
    rw = rw_ref[...]
    s = None
    for j in range(TOP_K):
        y_j = jnp.concatenate([y_ref[c, j] for c in range(y_ref.shape[0])], axis=1) * rw[:, j:j + 1]
        s = y_j if s is None else s + y_j
    x2 = x_ref[...] + gt2_ref[0] * s
    if final:
        x2 = _rms(x2) * gfin_ref[...]
    o_ref[...] = x2


def _combine_call(rows, n_tiles, x1, yg, rw, mod3, gfin, final):
    d = x1.shape[1]
    return pl.pallas_call(
        functools.partial(_combine_kernel, final=final),
        grid=(n_tiles,),
        in_specs=[pl.BlockSpec((TM, d), lambda i: (i, 0)),
                  pl.BlockSpec((Y_SLABS, TOP_K, TM, d // Y_SLABS), lambda i: (0, 0, i, 0)),
                  pl.BlockSpec((TM, LANES), lambda i: (i, 0)),
                  rows.mod_spec(5, d),
                  pl.BlockSpec((1, d), lambda i: (0, 0))],
        out_specs=pl.BlockSpec((TM, d), lambda i: (i, 0)),
        out_shape=jax.ShapeDtypeStruct((n_tiles * TM, d), F32),
        compiler_params=_cparams(("parallel",)),
        name="moe_combine_final" if final else "moe_combine",
    )(x1, yg, rw, mod3, gfin)


def _sc_mesh():
    return plsc.VectorSubcoreMesh(core_axis_name="core", subcore_axis_name="subcore")


def _sc_scatter_rows(src, idx_list, n_out):
    r, c = src.shape
    assert r % SC_WINDOW == 0

    @pl.kernel(out_type=jax.ShapeDtypeStruct((n_out, c), src.dtype), mesh=_sc_mesh(), scratch_types=[])
    def scatter(x_hbm, *refs):
        idx_hbm, o_hbm = refs[:-1], refs[-1]

        def body(x_vmem, *idx_vmem):
            for iv in idx_vmem:
                pltpu.sync_copy(x_vmem, o_hbm.at[iv.at[0]])

        ispec = pl.BlockSpec((1, SC_WINDOW), lambda i: (0, i))
        pltpu.emit_pipeline(
            body, grid=(r // SC_WINDOW,),
            in_specs=[pl.BlockSpec((SC_WINDOW, c), lambda i: (i, 0))] + [ispec] * len(idx_hbm),
            out_specs=[], core_axis_name=("core", "subcore"),
            dimension_semantics=(pltpu.PARALLEL,))(x_hbm, *idx_hbm)

    return scatter(src, *idx_list)


def _sc_gather_rows(src, idx):
    r, c = idx.shape[1], src.shape[1]
    assert r % SC_WINDOW == 0

    @pl.kernel(out_type=jax.ShapeDtypeStruct((r, c), src.dtype), mesh=_sc_mesh(), scratch_types=[])
    def gather(x_hbm, i_hbm, o_hbm):
        def body(i_vmem, o_vmem):
            pltpu.sync_copy(x_hbm.at[i_vmem.at[0]], o_vmem)

        pltpu.emit_pipeline(
            body, grid=(r // SC_WINDOW,),
            in_specs=[pl.BlockSpec((1, SC_WINDOW), lambda i: (0, i))],
            out_specs=[pl.BlockSpec((SC_WINDOW, c), lambda i: (i, 0))],
            core_axis_name=("core", "subcore"),
            dimension_semantics=(pltpu.PARALLEL,))(i_hbm, o_hbm)

    return gather(src, idx)


def _route(top_idx, rank, counts):
    t = top_idx.shape[0]
    m = t * TOP_K
    padded = ((counts + MOE_TB - 1) // MOE_TB) * MOE_TB
    pend = jnp.cumsum(padded)
    pstart = pend - padded
    onehot = top_idx[:, :, None] == jnp.arange(N_EXPERTS, dtype=jnp.int32)[None, None, :]
    dest = jnp.sum(jnp.where(onehot, pstart[None, None, :], 0), axis=-1) + rank
    n_blk = -(-m // MOE_TB) + N_EXPERTS
    n_used = (pend[-1] // MOE_TB).astype(jnp.int32)
    blk = jnp.arange(n_blk, dtype=jnp.int32)
    blk_e = jnp.sum((blk[:, None] * MOE_TB >= pend[None, :]).astype(jnp.int32), axis=1)
    blk_e = jnp.minimum(blk_e, N_EXPERTS - 1)
    e_hot = blk_e[:, None] == jnp.arange(N_EXPERTS, dtype=jnp.int32)[None, :]
    first = jnp.sum(jnp.where(e_hot, pstart[None, :], 0), axis=1) // MOE_TB
    cnt_e = jnp.sum(jnp.where(e_hot, counts[None, :], 0), axis=1)
    blk_valid = jnp.clip(cnt_e - (blk - first) * MOE_TB, 0, MOE_TB).astype(jnp.int32)
    last_e = jnp.sum(jnp.where(blk == n_used - 1, blk_e, 0))
    blk_e = jnp.where(blk < n_used, blk_e, last_e).astype(jnp.int32)
    return dest, blk_e, n_used.reshape(1), blk_valid, n_blk * MOE_TB


def _rope_table(n, rot_dim, lane_off, period):
    qd = rot_dim // 4
    pos = jnp.arange(n, dtype=jnp.int32)
    rws = (pos // GRID_W).astype(F32)
    cls = (pos % GRID_W).astype(F32)
    inv = ROPE_BASE ** (-jnp.arange(qd, dtype=F32) / qd)
    lane = jnp.arange(LANES, dtype=jnp.int32)
    w = (lane - lane_off) % period
    active = (lane >= lane_off) & (w < rot_dim) if period == LANES else jnp.ones((LANES,), bool)
    slot = w // qd
    freq = inv[w % qd]
    ang = jnp.where((slot < 2)[None, :], rws[:, None], cls[:, None]) * freq[None, :]
    sign = jnp.where(slot % 2 == 0, -1.0, 1.0)
    cos = jnp.where(active[None, :], jnp.cos(ang), 1.0)
    sin = jnp.where(active[None, :], jnp.sin(ang) * sign[None, :], 0.0)
    ident = (jnp.ones((TM, LANES), F32), jnp.zeros((TM, LANES), F32))
    return jnp.concatenate([cos, ident[0]], 0), jnp.concatenate([sin, ident[1]], 0)


def _prep_weights(p):
    w_in = p["w_in"]
    nl, d, _ = w_in.shape
    o = {}
    o["w_a"] = w_in[:, :, :3 * A_WIDTH].astype(BF16)
    o["w_b"] = w_in[:, :, 3 * A_WIDTH:3 * A_WIDTH + 3 * B_WIDTH].astype(BF16)
    c0 = 3 * A_WIDTH + 3 * B_WIDTH
    c1 = c0 + C_Q_RANK + C_KV_RANK
    zeros = lambda k: jnp.zeros((nl, d, k), F32)
    o["w_c"] = jnp.concatenate([w_in[:, :, c0:c1], zeros(C_NOPE), w_in[:, :, c1:c1 + C_ROPE],
                                zeros(LANES - C_NOPE - C_ROPE)], axis=-1).astype(BF16)
    o["w_g"] = w_in[:, :, c1 + C_ROPE:].astype(BF16)
    wq = p["w_q_b"].reshape(nl, C_Q_RANK, C_HEADS, C_NOPE + C_ROPE)
    wq = jnp.pad(wq, ((0, 0), (0, 0), (0, 0), (0, LANES - C_NOPE - C_ROPE)))
    o["w_q_b"] = wq.reshape(nl, C_Q_RANK, C_HEADS * LANES).astype(BF16)
    wkv = p["w_kv_b"].reshape(nl, C_KV_RANK, C_HEADS, C_NOPE + C_VDIM)
    wk = jnp.pad(wkv[..., :C_NOPE], ((0, 0), (0, 0), (0, 0), (0, LANES - C_NOPE)))
    o["w_kv_b"] = jnp.concatenate([wk.reshape(nl, C_KV_RANK, C_HEADS * LANES),
                                   wkv[..., C_NOPE:].reshape(nl, C_KV_RANK, C_WIDTH)], axis=-1).astype(BF16)
    for name in ("w_br_a", "w_br_b", "w_br_c", "w_out"):
        o[name] = p[name].astype(BF16)
    o["w_gate_up"] = p["w_gate_up"].reshape((-1,) + p["w_gate_up"].shape[2:])
    o["w_down"] = p["w_down"].reshape((-1,) + p["w_down"].shape[2:])
    o["b_gate_up"] = _regroup_gate_up(p["b_gate_up"]).reshape(-1, 1, p["b_gate_up"].shape[-1])
    o["b_down"] = p["b_down"].reshape(-1, 1, p["b_down"].shape[-1])
    o["w_router"] = jnp.pad(p["w_router"], ((0, 0), (0, 0), (0, LANES - N_EXPERTS)))
    o["b_router"] = jnp.pad(p["b_router"], ((0, 0), (0, LANES - N_EXPERTS)))[:, None, :]
    return o


@jax.jit
def _trunk(p):
    x, ctx = p["x"], p["ctx"]
    bsz, n, d = x.shape
    n_ctx = ctx.shape[1]
    depth = p["w_mod"].shape[0]
    rows = _Rows(bsz, n, n_ctx)
    w = _prep_weights(p)
    rope_a = _rope_table(n, A_HEAD_DIM, 0, A_HEAD_DIM)
    rope_c = _rope_table(n, C_ROPE, C_NOPE, LANES)
    scale_a = A_HEAD_DIM ** -0.5 * LOG2E
    scale_b = B_HEAD_DIM ** -0.5 * LOG2E
    scale_c = (C_NOPE + C_ROPE) ** -0.5 * LOG2E
    c8 = jnp.zeros((8, d), F32).at[:bsz].set(p["c"]).at[bsz].set(p["c_ctx"])
    x_all = jnp.concatenate([x.reshape(bsz * n, d), ctx.reshape(bsz * n_ctx, d)], axis=0)
    dummy_lam = jnp.zeros((4, A_HEAD_DIM), F32)
    dummy_g = jnp.ones((1, LANES), F32)
    out = None
    for l in range(depth):
        last = l == depth - 1
        lam_init = 0.8 - 0.6 * math.exp(-0.3 * l)
        mod3 = _mod_call(c8, p["w_mod"][l], p["b_mod"][l]).reshape(8 * N_MOD, 1, d)
        g_mix = p["g_mix"][l][None, :]
        qa, ka, va, qb, kb, vb, qc, kc, vc = _inproj_attn_call(
            rows, x_all, g_mix, mod3, w["w_a"][l], w["w_b"][l], w["w_c"][l], p["g_q_a"][l][None, :], w["w_q_b"][l],
            p["g_kv_a"][l][None, :], w["w_kv_b"][l], rope_a, rope_c, (scale_a, scale_b, scale_c))
        gates = _inproj_gates_call(rows, x_all, g_mix, mod3, w["w_g"][l])
        lamv = jnp.stack([p["lam_q1"][l], p["lam_k1"][l], p["lam_q2"][l], p["lam_k2"][l]])
        gs = p["g_subln"][l][None, :]
        attn = functools.partial(_pair_attention_call, rows)
        ya = attn(qa, ka, va, lamv, gs, wq=LANES, mode="diff", lam_init=lam_init, ctx_queries=False)
        yb = _nbr_attention_call(rows, qb, kb, vb, p["rpb"][l])
        yc = attn(qc, kc, vc, dummy_lam, dummy_g, wq=2 * LANES, mode="pair", lam_init=0.0, ctx_queries=False)
        if not last:
            ya_c = attn(qa, ka, va, lamv, gs, wq=LANES, mode="diff", lam_init=lam_init, ctx_queries=True)
            yb_c = attn(qb, kb, vb, dummy_lam, dummy_g, wq=LANES, mode="pair", lam_init=0.0, ctx_queries=True)
            yc_c = attn(qc, kc, vc, dummy_lam, dummy_g, wq=2 * LANES, mode="pair", lam_init=0.0, ctx_queries=True)
            ya, yb, yc = (jnp.concatenate(pair, axis=0) for pair in ((ya, ya_c), (yb, yb_c), (yc, yc_c)))
        n_tiles = rows.lat_tiles if last else rows.all_tiles
        x1, h2, ri, rw, rk, cnt = _merge_call(rows, n_tiles, ya, yb, yc, gates, x_all, w["w_br_a"][l], w["w_br_b"][l],
                                     w["w_br_c"][l], w["w_out"][l], mod3, p["g_ffn"][l][None, :],
                                     w["w_router"][l], w["b_router"][l])
        dest, blk_e, n_used, blk_valid, n_rows = _route(ri[:, :TOP_K], rk[:, :TOP_K],
                                                        cnt[0, :N_EXPERTS].astype(jnp.int32))
        t_rows = n_tiles * TM
        n_pieces = h2.shape[0]
        piece_off = (jnp.arange(n_pieces, dtype=jnp.int32) * n_rows)[:, None]
        xg = _sc_scatter_rows(h2.reshape(n_pieces * t_rows, LANES),
                              [(piece_off + dest[None, :, j]).reshape(1, -1) for j in range(TOP_K)],
                              n_pieces * n_rows).reshape(n_pieces, n_rows, LANES)
        y = _moe_call(xg, blk_e + l * N_EXPERTS, n_used, blk_valid, w["w_gate_up"], w["b_gate_up"],
                      w["w_down"], w["b_down"])
        slab_off = (jnp.arange(Y_SLABS, dtype=jnp.int32) * n_rows)[:, None, None]
        yg = _sc_gather_rows(y.reshape(Y_SLABS * n_rows, d // Y_SLABS),
                             (slab_off + dest.T[None]).reshape(1, -1)).reshape(Y_SLABS, TOP_K, t_rows, d // Y_SLABS)
        x_new = _combine_call(rows, n_tiles, x1, yg, rw, mod3, p["g_final"][None, :], last)
        if last:
            out = x_new.reshape(bsz, n, d)
        else:
            x_all = x_new
    return out


def kernel(x, c, ctx, c_ctx, w_mod, b_mod, g_mix, w_in, lam_q1, lam_k1, lam_q2, lam_k2, g_subln, rpb, g_q_a, w_q_b, g_kv_a, w_kv_b, w_br_a, w_br_b, w_br_c, w_out, g_ffn, w_router, b_router, w_gate_up, b_gate_up, w_down, b_down, g_final):
    return _trunk(dict(x=x, c=c, ctx=ctx, c_ctx=c_ctx, w_mod=w_mod, b_mod=b_mod, g_mix=g_mix, w_in=w_in,
                       lam_q1=lam_q1, lam_k1=lam_k1, lam_q2=lam_q2, lam_k2=lam_k2, g_subln=g_subln, rpb=rpb,
                       g_q_a=g_q_a, w_q_b=w_q_b, g_kv_a=g_kv_a, w_kv_b=w_kv_b, w_br_a=w_br_a, w_br_b=w_br_b,
                       w_br_c=w_br_c, w_out=w_out, g_ffn=g_ffn, w_router=w_router, b_router=b_router,
                       w_gate_up=w_gate_up, b_gate_up=b_gate_up, w_down=w_down, b_down=b_down, g_final=g_final))
```
